```python
import numpy as np
import jax
import jax.numpy as jnp
from jax import lax

D_MODEL = 1024
BATCH = 16
SEQ = 256
DEPTH = 2
DEC_BATCH = 8
DEC_SEQ = 4096
PAST_LEN = 256

GRID_W = 64
CHUNK = 64
MIX = D_MODEL
GROUP_W = MIX // 4
H_M = 4
DH_M = GROUP_W // H_M
H_R = 4
DH_R = GROUP_W // H_R
H_G = 4
DV_G = GROUP_W // H_G
DK_G = DV_G // 2
GLA_RANK = 16
GLA_TAU = 16.0
H_S = 4
P_S = GROUP_W // H_S
N_S = 64
G_S = 2
CONV_W = 3
CONV_CH = GROUP_W + 2 * G_S * N_S
N_EXPERTS = 32
TOP_K = 4
D_FF = D_MODEL
SWIGLU_LIMIT = 7.0
SWIGLU_ALPHA = 1.702
MOE_BLOCK = 128
ROPE_BASE = 10000.0
EPS = 1e-6
IN_SIZES = (GROUP_W, GROUP_W, GROUP_W, GROUP_W, 2 * H_M, 2 * H_M,
            GROUP_W, GROUP_W, GROUP_W, GROUP_W,
            H_G * DK_G, H_G * DK_G, GROUP_W, GROUP_W, 2 * GLA_RANK,
            GROUP_W, CONV_CH, 2 * H_S)
IN_DIM = sum(IN_SIZES)

kernel_name = 'hybrid_flow_mlstm_ret_gla_ssd_moe'


def rmsnorm(x, g):
    xf = x.astype(jnp.float32)
    y = xf * lax.rsqrt(jnp.mean(xf * xf, axis=-1, keepdims=True) + EPS)
    return y.astype(x.dtype) * g


def head_norm(t, n_heads, g):
    b_, l_, w_ = t.shape
    th = t.reshape(b_, l_, n_heads, w_ // n_heads)
    th = th * lax.rsqrt(jnp.mean(th * th, axis=-1, keepdims=True) + EPS)
    return th.reshape(b_, l_, w_) * g.astype(jnp.float32)


def to_heads(t, n_heads):
    b_, l_, w_ = t.shape
    return t.reshape(b_, l_, n_heads, w_ // n_heads).transpose(0, 2, 1, 3).astype(jnp.float32)


def merge_heads(t):
    b_, h_, l_, d_ = t.shape
    return t.transpose(0, 2, 1, 3).reshape(b_, l_, h_ * d_)


def orient(t, d):
    return jnp.flip(t, axis=2) if d == 1 else t


def to_chunks(t):
    b_, h_, l_ = t.shape[:3]
    t = t.reshape((b_, h_, l_ // CHUNK, CHUNK) + t.shape[3:])
    return jnp.moveaxis(t, 2, 0)


def from_chunks(t):
    t = jnp.moveaxis(t, 0, 2)
    return t.reshape((t.shape[0], t.shape[1], t.shape[2] * t.shape[3]) + t.shape[4:])


def chunk_mask():
    return jnp.tril(jnp.ones((CHUNK, CHUNK), dtype=bool))


def scan_scalar_decay(q, k, v, log_a, s0):
    mask = chunk_mask()

    def step(S, xs):
        qc, kc, vc, ac = xs
        b = jnp.cumsum(ac, axis=-1)
        dec = jnp.exp(jnp.where(mask, b[..., :, None] - b[..., None, :], -jnp.inf))
        sc = jnp.einsum('bhid,bhjd->bhij', qc, kc) * dec
        o = jnp.einsum('bhij,bhje->bhie', sc, vc) + jnp.exp(b)[..., None] * jnp.einsum('bhid,bhde->bhie', qc, S)
        bl = b[..., -1]
        S = jnp.exp(bl)[..., None, None] * S + jnp.einsum('bhjd,bhje->bhde', kc * jnp.exp(bl[..., None] - b)[..., None], vc)
        return S, o

    S, o = lax.scan(step, s0.astype(jnp.float32), (to_chunks(q), to_chunks(k), to_chunks(v), to_chunks(log_a)))
    return from_chunks(o), S


def scan_vector_decay(q, k, v, log_a, s0):
    mask = chunk_mask()

    def step(S, xs):
        qc, kc, vc, ac = xs
        b = jnp.cumsum(ac, axis=-2)
        qe = qc * jnp.exp(b)
        ke = kc * jnp.exp(-b)
        sc = jnp.where(mask, jnp.einsum('bhid,bhjd->bhij', qe, ke), 0.0)
        o = jnp.einsum('bhij,bhje->bhie', sc, vc) + jnp.einsum('bhid,bhde->bhie', qe, S)
        bl = b[..., -1, :]
        S = jnp.exp(bl)[..., None] * S + jnp.einsum('bhjd,bhje->bhde', kc * jnp.exp(bl[..., None, :] - b), vc)
        return S, o

    S, o = lax.scan(step, s0.astype(jnp.float32), (to_chunks(q), to_chunks(k), to_chunks(v), to_chunks(log_a)))
    return from_chunks(o), S


def mlstm_scan(q, k, v, ig, lf, C0, n0, m0):
    mask = chunk_mask()

    def step(carry, xs):
        Cs, ns, ms = carry
        qc, kc, vc, ic, fc = xs
        b = jnp.cumsum(fc, axis=-1)
        d_log = jnp.where(mask, b[..., :, None] - b[..., None, :] + ic[..., None, :], -jnp.inf)
        inter = b + ms[..., None]
        m_t = jnp.maximum(inter, jnp.max(d_log, axis=-1))
        w = jnp.exp(d_log - m_t[..., None])
        sc = jnp.einsum('bhid,bhjd->bhij', qc, kc) * w
        w_inter = jnp.exp(inter - m_t)
        num = jnp.einsum('bhij,bhje->bhie', sc, vc) + w_inter[..., None] * jnp.einsum('bhid,bhde->bhie', qc, Cs)
        den = jnp.sum(sc, axis=-1) + w_inter * jnp.einsum('bhid,bhd->bhi', qc, ns)
        out = num / jnp.maximum(jnp.abs(den), jnp.exp(-m_t))[..., None]
        bl = b[..., -1]
        g_log = bl[..., None] - b + ic
        m_new = jnp.maximum(bl + ms, jnp.max(g_log, axis=-1))
        wk = jnp.exp(g_log - m_new[..., None])
        decay = jnp.exp(bl + ms - m_new)
        Cs = decay[..., None, None] * Cs + jnp.einsum('bhj,bhjd,bhje->bhde', wk, kc, vc)
        ns = decay[..., None] * ns + jnp.einsum('bhj,bhjd->bhd', wk, kc)
        return (Cs, ns, m_new), out

    init = (C0.astype(jnp.float32), n0.astype(jnp.float32), m0.astype(jnp.float32))
    xs = (to_chunks(q), to_chunks(k), to_chunks(v), to_chunks(ig), to_chunks(lf))
    (Cf, nf, mf), o = lax.scan(step, init, xs)
    return from_chunks(o), (Cf, nf, mf)


def grid_rope(n_tokens, head_dim):
    rows = n_tokens // GRID_W
    r, col = jnp.meshgrid(jnp.arange(rows), jnp.arange(GRID_W), indexing='ij')
    nf = head_dim // 4
    inv = ROPE_BASE ** (-jnp.arange(nf, dtype=jnp.float32) / nf)
    ang = jnp.concatenate([r.reshape(-1, 1).astype(jnp.float32) * inv,
                           col.reshape(-1, 1).astype(jnp.float32) * inv], axis=-1)
    return jnp.cos(ang), jnp.sin(ang)


def apply_rope(x, cos, sin):
    half = x.shape[-1] // 2
    x1, x2 = x[..., :half], x[..., half:]
    return jnp.concatenate([x1 * cos - x2 * sin, x1 * sin + x2 * cos], axis=-1)


def conv_centred(x, w, b):
    y = lax.conv_general_dilated(x, w[:, None, :].astype(x.dtype), window_strides=(1,), padding='SAME',
                                 dimension_numbers=('NWC', 'WIO', 'NWC'), feature_group_count=x.shape[-1])
    return y + b


def token_mixers(h, init, lp, rope):
    b_, l_, _ = h.shape
    offs = np.cumsum(IN_SIZES)[:-1].tolist()
    (mq, mk, mv, mo, mi, mf, rq, rk, rv, rg, gq, gk, gv, gr, glr, sz, sxbc, sdt) = jnp.split(h @ lp['w_in'], offs, axis=-1)
    mC0, mn0, mm0, r0, g0, s0 = init

    q = to_heads(mq, H_M) * DH_M ** -0.5
    k = to_heads(mk, H_M)
    v = to_heads(mv, H_M)
    i_pre = mi.reshape(b_, l_, 2, H_M).astype(jnp.float32)
    f_pre = mf.reshape(b_, l_, 2, H_M).astype(jnp.float32)
    m_outs, m_fin = [], []
    for d in range(2):
        ig = jnp.moveaxis(i_pre[:, :, d] + lp['mlstm_i_b'][d], 1, 2)
        lf = jax.nn.log_sigmoid(jnp.moveaxis(f_pre[:, :, d] + lp['mlstm_f_b'][d], 1, 2))
        o_d, st = mlstm_scan(orient(q, d), orient(k, d), orient(v, d), orient(ig, d), orient(lf, d),
                             mC0[:, d], mn0[:, d], mm0[:, d])
        m_outs.append(orient(o_d, d))
        m_fin.append(st)
    m_out = head_norm(merge_heads(m_outs[0] + m_outs[1]), H_M, lp['mlstm_norm_g']) * jax.nn.sigmoid(mo.astype(jnp.float32))

    q = to_heads(rq, H_R)
    k = to_heads(rk, H_R) * DH_R ** -0.5
    v = to_heads(rv, H_R)
    if rope is not None:
        q = apply_rope(q, rope[0], rope[1])
        k = apply_rope(k, rope[0], rope[1])
    r_outs, r_fin = [], []
    for d in range(2):
        log_a = jnp.broadcast_to(-jnp.exp(lp['ret_decay'][d].astype(jnp.float32))[None, :, None], (b_, H_R, l_))
        o_d, st = scan_scalar_decay(orient(q, d), orient(k, d), orient(v, d), orient(log_a, d), r0[:, d])
        r_outs.append(orient(o_d, d))
        r_fin.append(st)
    r_out = head_norm(merge_heads(r_outs[0] + r_outs[1]), H_R, lp['ret_norm_g']) * jax.nn.silu(rg.astype(jnp.float32))

    q = to_heads(gq, H_G)
    k = to_heads(gk, H_G) * DK_G ** -0.5
    v = to_heads(gv, H_G)
    lr = glr.reshape(b_, l_, 2, GLA_RANK).astype(jnp.float32)
    g_outs, g_fin = [], []
    for d in range(2):
        gate = lr[:, :, d] @ lp['gla_gate_w'][d].astype(jnp.float32) + lp['gla_gate_b'][d]
        log_a = to_heads(jax.nn.log_sigmoid(gate) / GLA_TAU, H_G)
        o_d, st = scan_vector_decay(orient(q, d), orient(k, d), orient(v, d), orient(log_a, d), g0[:, d])
        g_outs.append(orient(o_d, d))
        g_fin.append(st)
    g_out = head_norm(merge_heads(g_outs[0] + g_outs[1]), H_G, lp['gla_norm_g']) * jax.nn.silu(gr.astype(jnp.float32))

    xbc = jax.nn.silu(conv_centred(sxbc, lp['ssd_conv_w'], lp['ssd_conv_b']))
    sx, sB, sC = jnp.split(xbc, [GROUP_W, GROUP_W + G_S * N_S], axis=-1)
    x_h = to_heads(sx, H_S)
    B_h = jnp.repeat(to_heads(sB, G_S), H_S // G_S, axis=1)
    C_h = jnp.repeat(to_heads(sC, G_S), H_S // G_S, axis=1)
    dt_pre = sdt.reshape(b_, l_, 2, H_S).astype(jnp.float32)
    s_outs, s_fin = [], []
    for d in range(2):
        dt = jax.nn.softplus(jnp.moveaxis(dt_pre[:, :, d] + lp['ssd_dt_b'][d], 1, 2))
        log_a = dt * (-jnp.exp(lp['ssd_A_log'][d].astype(jnp.float32)))[None, :, None]
        o_d, st = scan_scalar_decay(orient(C_h, d), orient(B_h, d), orient(x_h * dt[..., None], d), orient(log_a, d), s0[:, d])
        s_outs.append(orient(o_d, d))
        s_fin.append(st)
    y_s = s_outs[0] + s_outs[1] + lp['ssd_D'].astype(jnp.float32)[None, :, None, None] * x_h
    s_out = head_norm(merge_heads(y_s) * jax.nn.silu(sz.astype(jnp.float32)), H_S, lp['ssd_norm_g'])

    mixed = jnp.concatenate([m_out, r_out, g_out, s_out], axis=-1).astype(h.dtype) @ lp['w_out']
    finals = (jnp.stack([m_fin[0][0], m_fin[1][0]], axis=1),
              jnp.stack([m_fin[0][1], m_fin[1][1]], axis=1),
              jnp.stack([m_fin[0][2], m_fin[1][2]], axis=1),
              jnp.stack(r_fin, axis=1), jnp.stack(g_fin, axis=1), jnp.stack(s_fin, axis=1))
    return mixed, finals


def moe_ffn(h, lp):
    b_, l_, d_ = h.shape
    x = h.reshape(-1, d_)
    t_ = x.shape[0]
    logits = (x @ lp['router_w'] + lp['router_b']).astype(jnp.float32)
    top_val, top_idx = lax.top_k(logits, TOP_K)
    gates = jax.nn.softmax(top_val, axis=-1)
    n_assign = t_ * TOP_K
    flat_e = top_idx.reshape(-1)
    order = jnp.argsort(flat_e)
    e_sorted = flat_e[order]
    tok_sorted = (order // TOP_K).astype(jnp.int32)
    gate_sorted = gates.reshape(-1)[order]
    counts = jnp.bincount(flat_e, length=N_EXPERTS)
    padded = (counts + MOE_BLOCK - 1) // MOE_BLOCK * MOE_BLOCK
    pad_end = jnp.cumsum(padded)
    pad_start = pad_end - padded
    start = jnp.cumsum(counts) - counts
    dest = pad_start[e_sorted] + jnp.arange(n_assign) - start[e_sorted]
    n_blocks = (n_assign + MOE_BLOCK - 1) // MOE_BLOCK + N_EXPERTS
    rows = n_blocks * MOE_BLOCK
    src = jnp.full((rows,), t_, dtype=jnp.int32).at[dest].set(tok_sorted)
    block_e = jnp.minimum(jnp.searchsorted(pad_end, jnp.arange(n_blocks) * MOE_BLOCK, side='right'), N_EXPERTS - 1)
    x_pad = jnp.concatenate([x, jnp.zeros((1, d_), x.dtype)], axis=0)
    xb = x_pad[src].reshape(n_blocks, MOE_BLOCK, d_)
    w_gu, b_gu, w_dn, b_dn = lp['moe_w_gu'], lp['moe_b_gu'], lp['moe_w_down'], lp['moe_b_down']

    def expert_block(args):
        xblk, e = args
        gu = xblk @ w_gu[e] + b_gu[e]
        gate, up = gu[:, :D_FF], gu[:, D_FF:]
        gate = jnp.minimum(gate, SWIGLU_LIMIT)
        up = jnp.clip(up, -SWIGLU_LIMIT, SWIGLU_LIMIT)
        act = gate * jax.nn.sigmoid(SWIGLU_ALPHA * gate) * (up + 1.0)
        return act @ w_dn[e] + b_dn[e]

    yb = lax.map(expert_block, (xb, block_e)).reshape(rows, d_)
    y = jax.ops.segment_sum(yb[dest] * gate_sorted[:, None].astype(yb.dtype), tok_sorted, num_segments=t_)
    return y.reshape(b_, l_, d_)


def layer_forward(x, cond, init, lp, rope):
    mod = (jax.nn.silu(cond) @ lp['ada_w'] + lp['ada_b'])[:, None, :]
    sh1, sc1, g1, sh2, sc2, g2 = jnp.split(mod, 6, axis=-1)
    h = rmsnorm(x, lp['norm1_g']) * (1.0 + sc1) + sh1
    mixed, finals = token_mixers(h, init, lp, rope)
    x = x + g1 * mixed
    h = rmsnorm(x, lp['norm2_g']) * (1.0 + sc2) + sh2
    x = x + g2 * moe_ffn(h, lp)
    return x, finals


def setup_inputs(seed: int = 0) -> dict:
    key = jax.random.key(seed)
    ks = iter(jax.random.split(key, 48))

    def nrm(shape, scale):
        return scale * jax.random.normal(next(ks), shape, dtype=jnp.float32)

    def gain(shape):
        return 1.0 + nrm(shape, 0.02)

    sd = (DEC_BATCH, DEPTH, 2)
    inp = {}
    inp['x_prompt'] = nrm((BATCH, SEQ, D_MODEL), 1.0)
    inp['x_sample'] = nrm((DEC_BATCH, DEC_SEQ, D_MODEL), 1.0)
    inp['c'] = nrm((DEC_BATCH, D_MODEL), 1.0)
    inp['state_mlstm_C'] = nrm(sd + (H_M, DH_M, DH_M), 0.3)
    inp['state_mlstm_n'] = nrm(sd + (H_M, DH_M), 0.3)
    inp['state_mlstm_m'] = nrm(sd + (H_M,), 0.5)
    inp['state_ret'] = nrm(sd + (H_R, DH_R, DH_R), 0.3)
    inp['state_gla'] = nrm(sd + (H_G, DK_G, DV_G), 0.3)
    inp['state_ssd'] = nrm(sd + (H_S, N_S, P_S), 0.3)
    inp['c_ctx'] = nrm((D_MODEL,), 1.0)
    inp['ada_w'] = nrm((DEPTH, D_MODEL, 6 * D_MODEL), 0.5 * D_MODEL ** -0.5)
    inp['ada_b'] = nrm((DEPTH, 6 * D_MODEL), 0.02)
    inp['norm1_g'] = gain((DEPTH, D_MODEL))
    inp['norm2_g'] = gain((DEPTH, D_MODEL))
    inp['final_g'] = gain((D_MODEL,))
    inp['w_in'] = nrm((DEPTH, D_MODEL, IN_DIM), D_MODEL ** -0.5)
    inp['w_out'] = nrm((DEPTH, MIX, D_MODEL), MIX ** -0.5)
    inp['mlstm_i_b'] = nrm((DEPTH, 2, H_M), 0.1)
    inp['mlstm_f_b'] = jnp.linspace(3.0, 6.0, H_M, dtype=jnp.float32)[None, None, :] + nrm((DEPTH, 2, H_M), 0.1)
    inp['mlstm_norm_g'] = gain((DEPTH, GROUP_W))
    base_decay = jnp.log(-jnp.log1p(-(2.0 ** (-5.0 - jnp.arange(H_R, dtype=jnp.float32)))))
    inp['ret_decay'] = base_decay[None, None, :] + nrm((DEPTH, 2, H_R), 0.05)
    inp['ret_norm_g'] = gain((DEPTH, GROUP_W))
    inp['gla_gate_w'] = nrm((DEPTH, 2, GLA_RANK, H_G * DK_G), GLA_RANK ** -0.5)
    inp['gla_gate_b'] = nrm((DEPTH, 2, H_G * DK_G), 0.1)
    inp['gla_norm_g'] = gain((DEPTH, GROUP_W))
    inp['ssd_conv_w'] = nrm((DEPTH, CONV_W, CONV_CH), CONV_W ** -0.5)
    inp['ssd_conv_b'] = nrm((DEPTH, CONV_CH), 0.02)
    dt0 = jnp.exp(jax.random.uniform(next(ks), (DEPTH, 2, H_S), minval=float(np.log(1e-3)), maxval=float(np.log(1e-1))))
    inp['ssd_dt_b'] = dt0 + jnp.log(-jnp.expm1(-dt0))
    inp['ssd_A_log'] = jnp.log(jax.random.uniform(next(ks), (DEPTH, 2, H_S), minval=1.0, maxval=16.0))
    inp['ssd_D'] = 1.0 + nrm((DEPTH, H_S), 0.1)
    inp['ssd_norm_g'] = gain((DEPTH, GROUP_W))
    inp['router_w'] = nrm((DEPTH, D_MODEL, N_EXPERTS), D_MODEL ** -0.5)
    inp['router_b'] = nrm((DEPTH, N_EXPERTS), 0.01)
    inp['moe_w_gu'] = nrm((DEPTH, N_EXPERTS, D_MODEL, 2 * D_FF), D_MODEL ** -0.5)
    inp['moe_b_gu'] = nrm((DEPTH, N_EXPERTS, 2 * D_FF), 0.02)
    inp['moe_w_down'] = nrm((DEPTH, N_EXPERTS, D_FF, D_MODEL), D_FF ** -0.5)
    inp['moe_b_down'] = nrm((DEPTH, N_EXPERTS, D_MODEL), 0.02)
    return inp


def reference(x_prompt, x_sample, c, state_mlstm_C, state_mlstm_n, state_mlstm_m, state_ret, state_gla, state_ssd,
              c_ctx, ada_w, ada_b, norm1_g, norm2_g, final_g, w_in, w_out,
              mlstm_i_b, mlstm_f_b, mlstm_norm_g, ret_decay, ret_norm_g,
              gla_gate_w, gla_gate_b, gla_norm_g,
              ssd_conv_w, ssd_conv_b, ssd_dt_b, ssd_A_log, ssd_D, ssd_norm_g,
              router_w, router_b, moe_w_gu, moe_b_gu, moe_w_down, moe_b_down):
    layers = [dict(ada_w=ada_w[l], ada_b=ada_b[l], norm1_g=norm1_g[l], norm2_g=norm2_g[l],
                   w_in=w_in[l], w_out=w_out[l],
                   mlstm_i_b=mlstm_i_b[l], mlstm_f_b=mlstm_f_b[l], mlstm_norm_g=mlstm_norm_g[l],
                   ret_decay=ret_decay[l], ret_norm_g=ret_norm_g[l],
                   gla_gate_w=gla_gate_w[l], gla_gate_b=gla_gate_b[l], gla_norm_g=gla_norm_g[l],
                   ssd_conv_w=ssd_conv_w[l], ssd_conv_b=ssd_conv_b[l], ssd_dt_b=ssd_dt_b[l],
                   ssd_A_log=ssd_A_log[l], ssd_D=ssd_D[l], ssd_norm_g=ssd_norm_g[l],
                   router_w=router_w[l], router_b=router_b[l],
                   moe_w_gu=moe_w_gu[l], moe_b_gu=moe_b_gu[l], moe_w_down=moe_w_down[l], moe_b_down=moe_b_down[l])
              for l in range(DEPTH)]

    bp = x_prompt.shape[0]
    f32 = jnp.float32
    zero_init = (jnp.zeros((bp, 2, H_M, DH_M, DH_M), f32), jnp.zeros((bp, 2, H_M, DH_M), f32),
                 jnp.zeros((bp, 2, H_M), f32), jnp.zeros((bp, 2, H_R, DH_R, DH_R), f32),
                 jnp.zeros((bp, 2, H_G, DK_G, DV_G), f32), jnp.zeros((bp, 2, H_S, N_S, P_S), f32))
    y = x_prompt
    ctx_finals = []
    for lp in layers:
        y, fin = layer_forward(y, c_ctx[None, :], zero_init, lp, None)
        ctx_finals.append(fin)
    y_prompt = rmsnorm(y, final_g)
    new_mlstm_C = jnp.stack([f[0] for f in ctx_finals], axis=1)
    new_mlstm_n = jnp.stack([f[1] for f in ctx_finals], axis=1)
    new_mlstm_m = jnp.stack([f[2] for f in ctx_finals], axis=1)
    new_ret = jnp.stack([f[3] for f in ctx_finals], axis=1)
    new_gla = jnp.stack([f[4] for f in ctx_finals], axis=1)
    new_ssd = jnp.stack([f[5] for f in ctx_finals], axis=1)

    rope = grid_rope(x_sample.shape[1], DH_R)
    z = x_sample
    for l, lp in enumerate(layers):
        init = (state_mlstm_C[:, l], state_mlstm_n[:, l], state_mlstm_m[:, l],
                state_ret[:, l], state_gla[:, l], state_ssd[:, l])
        z, _ = layer_forward(z, c, init, lp, rope)
    y_sample = rmsnorm(z, final_g)
    return (y_prompt, y_sample, new_mlstm_C, new_mlstm_n, new_mlstm_m, new_ret, new_gla, new_ssd)
```

```python
import functools

import numpy as np
import jax
import jax.numpy as jnp
from jax import lax
from jax.experimental import pallas as pl
from jax.experimental.pallas import tpu as pltpu

F32 = jnp.float32
BF16 = jnp.bfloat16

EPS = 1e-6
CHUNK = 64
GRID_W = 64
ROPE_BASE = 10000.0
GLA_TAU = 16.0
TOP_K = 4
SWIGLU_LIMIT = 7.0
SWIGLU_ALPHA = 1.702
LANES = 128
SUBLANES = 8
VMEM_LIMIT = 56 * 1024 * 1024

MQ, MK, MV = 0, 256, 512
RQ, RK, RV = 768, 1024, 1280
GQ, GK, GV = 1536, 1664, 1792
SXBC = 2048
SMALL = 2560
P_MAIN = 2688
S_MI, S_MF, S_DT, S_DTRAW, S_GLR = 0, 8, 16, 24, 32


def _softplus(x):
    return jnp.maximum(x, 0.0) + jnp.log1p(jnp.exp(-jnp.abs(x)))


def _sigmoid(x):
    return 1.0 / (1.0 + jnp.exp(-x))


def _dot(a, b):
    return jnp.dot(a.astype(BF16), b.astype(BF16), preferred_element_type=F32)


def _dot_nt(a, b):
    return lax.dot_general(a.astype(BF16), b.astype(BF16), (((1,), (1,)), ((), ())),
                           preferred_element_type=F32)


def _split3(x):
    x1 = x.astype(BF16)
    r1 = x - x1.astype(F32)
    x2 = r1.astype(BF16)
    r2 = r1 - x2.astype(F32)
    return x1, x2, r2.astype(BF16)


def _mask_dot(mask_bf16, x):
    x1, x2, x3 = _split3(x)
    d = functools.partial(jnp.dot, preferred_element_type=F32)
    return (d(mask_bf16, x3) + d(mask_bf16, x2)) + d(mask_bf16, x1)


def _dot_mask(x, mask_bf16):
    x1, x2, x3 = _split3(x)
    d = functools.partial(jnp.dot, preferred_element_type=F32)
    return (d(x3, mask_bf16) + d(x2, mask_bf16)) + d(x1, mask_bf16)


def _dot_hi(a, b_hi, b_lo):
    a1 = a.astype(BF16)
    a2 = (a - a1.astype(F32)).astype(BF16)
    d = functools.partial(jnp.dot, preferred_element_type=F32)
    return (d(a1, b_lo) + d(a2, b_hi)) + d(a1, b_hi)


def _ada_kernel(c_ref, w_ref, b_ref, o_ref):
    c = c_ref[...]
    s = c * _sigmoid(c)
    w = w_ref[0]
    w_hi = w.astype(BF16)
    w_lo = (w - w_hi.astype(F32)).astype(BF16)
    o_ref[0] = _dot_hi(s, w_hi, w_lo) + b_ref[0]


def _ada_mod(conds, ada_w, ada_b):
    depth, d, n6 = ada_w.shape
    rows = conds.shape[0]
    tn = 1536
    return pl.pallas_call(
        _ada_kernel,
        grid=(depth, n6 // tn),
        in_specs=[pl.BlockSpec((rows, d), lambda l, j: (0, 0)),
                  pl.BlockSpec((1, d, tn), lambda l, j: (l, 0, j)),
                  pl.BlockSpec((1, 1, tn), lambda l, j: (l, 0, j))],
        out_specs=pl.BlockSpec((1, rows, tn), lambda l, j: (l, 0, j)),
        out_shape=jax.ShapeDtypeStruct((depth, rows, n6), F32),
        compiler_params=pltpu.CompilerParams(dimension_semantics=("arbitrary", "arbitrary"),
                                             vmem_limit_bytes=VMEM_LIMIT),
        name="ada_mod",
    )(conds, ada_w, ada_b.reshape(depth, 1, n6))


def _proj_kernel(x_ref, mod_ref, g_ref, wm_ref, wg_ref, pm_ref, pg_ref):
    x = x_ref[...]
    y = x * lax.rsqrt(jnp.mean(x * x, axis=-1, keepdims=True) + EPS)
    h = (y * g_ref[...]) * (1.0 + mod_ref[0, 1:2, :]) + mod_ref[0, 0:1, :]
    hb = h.astype(BF16)
    pm_ref[...] = jnp.dot(hb, wm_ref[...], preferred_element_type=F32)
    pg_ref[...] = jnp.dot(hb, wg_ref[...], preferred_element_type=F32)


def _proj(x, mod, g, w_main, w_gate, seq_len, tm):
    t, d = x.shape
    tiles_per_seq = seq_len // tm
    n_mod = mod.shape[0]
    mod_map = (lambda i: (i // tiles_per_seq, 0, 0)) if n_mod > 1 else (lambda i: (0, 0, 0))
    return pl.pallas_call(
        _proj_kernel,
        grid=(t // tm,),
        in_specs=[pl.BlockSpec((tm, d), lambda i: (i, 0)),
                  pl.BlockSpec((1, 8, d), mod_map),
                  pl.BlockSpec((1, d), lambda i: (0, 0)),
                  pl.BlockSpec(w_main.shape, lambda i: (0, 0)),
                  pl.BlockSpec(w_gate.shape, lambda i: (0, 0))],
        out_specs=[pl.BlockSpec((tm, w_main.shape[1]), lambda i: (i, 0)),
                   pl.BlockSpec((tm, w_gate.shape[1]), lambda i: (i, 0))],
        out_shape=[jax.ShapeDtypeStruct((t, w_main.shape[1]), F32),
                   jax.ShapeDtypeStruct((t, w_gate.shape[1]), F32)],
        compiler_params=pltpu.CompilerParams(dimension_semantics=("arbitrary",),
                                             vmem_limit_bytes=VMEM_LIMIT),
        name="in_proj",
    )(x, mod, g, w_main, w_gate)


def _mixer_direction(d, c_idx, n_chunks, use_rope, pm, hprev, hnext, cos_ref, sin_ref,
                     small_b, a_row, ret_row, wg, gb, conv_w, conv_b, d_row,
                     o_ref, mC, mm, rS, gS, sS, la_s, lat_s, cs_s, cst_s, sp_s):
    C = CHUNK
    last = C - 1 if d == 0 else 0
    row = lax.broadcasted_iota(jnp.int32, (C, C), 0)
    col = lax.broadcasted_iota(jnp.int32, (C, C), 1)
    mask = (col <= row) if d == 0 else (col >= row)
    mask_t = (row <= col) if d == 0 else (row >= col)
    tri = mask.astype(BF16)
    tri_t = mask_t.astype(BF16)
    neg_inf = jnp.float32(-jnp.inf)

    small = pm[:, SMALL:SMALL + LANES]
    pre = small + small_b[...]
    lane = lax.broadcasted_iota(jnp.int32, (1, LANES), 1)
    sp = _softplus(pre)
    lsg = -_softplus(-pre)
    a_neg = -jnp.exp(a_row[...])
    la = jnp.where(lane < S_MF, pre,
                   jnp.where(lane < S_DT, lsg,
                             jnp.where(lane < S_DTRAW, sp * a_neg, 0.0)))
    la_t = la.T
    la_s[d] = la
    lat_s[d] = la_t
    cs_s[d] = _mask_dot(tri, la)
    cst_s[d] = _dot_mask(la_t, tri_t)
    sp_s[d] = sp

    q_all = pm[:, MQ:MQ + 256] * (64.0 ** -0.5)
    k_all = pm[:, MK:MK + 256]
    v_all = pm[:, MV:MV + 256]
    k_t = k_all.T
    ones_col = (lax.broadcasted_iota(jnp.int32, (C, 64), 1) == 0).astype(F32)
    for h in range(4):
        ci = d * 4 + h
        sl = slice(64 * h, 64 * h + 64)
        b_col = cs_s[d, :, S_MF + ci:S_MF + ci + 1]
        b_row = cst_s[d, S_MF + ci:S_MF + ci + 1, :]
        bl = cst_s[d, S_MF + ci:S_MF + ci + 1, last:last + 1]
        u_row = lat_s[d, S_MI + ci:S_MI + ci + 1, :] - b_row
        ms = mm[0, ci:ci + 1, 0:1]
        d_log = jnp.where(mask, b_col + u_row, neg_inf)
        inter = b_col + ms
        m_t = jnp.maximum(inter, jnp.max(d_log, axis=1, keepdims=True))
        w = jnp.exp(d_log - m_t)
        q_h = q_all[:, sl]
        sc = _dot_nt(q_h, k_all[:, sl]) * w
        w_inter = jnp.exp(inter - m_t)
        v_aug = jnp.concatenate([v_all[:, sl], ones_col], axis=1)
        c_aug = mC[0, d, h]
        num = _dot(sc, v_aug) + w_inter * _dot(q_h, c_aug)
        den = num[:, 64:65]
        o_ref[:, sl] = num[:, 0:64] / jnp.maximum(jnp.abs(den), jnp.exp(-m_t))
        g_row = bl + u_row
        m_new = jnp.maximum(bl + ms, jnp.max(g_row, axis=1, keepdims=True))
        wk_row = jnp.exp(g_row - m_new)
        decay = jnp.exp(bl + ms - m_new)
        mC[0, d, h] = decay * c_aug + _dot(k_t[sl, :] * wk_row, v_aug)
        mm[0, ci:ci + 1, :] = jnp.broadcast_to(m_new, (1, LANES))

    rq = pm[:, RQ:RQ + 256]
    rk = pm[:, RK:RK + 256] * (64.0 ** -0.5)
    rv = pm[:, RV:RV + 256]
    if use_rope:
        cosv = cos_ref[...]
        sinv = sin_ref[...]
        first = (lax.broadcasted_iota(jnp.int32, (1, 256), 1) % 64) < 32

        def rope(x):
            rot = jnp.where(first, pltpu.roll(x, 256 - 32, 1), pltpu.roll(x, 32, 1))
            return x * cosv + rot * sinv

        rq = rope(rq)
        rk = rope(rk)
    rk_t = rk.T
    adiff = jnp.abs(row - col).astype(F32)
    ci_f = lax.broadcasted_iota(jnp.int32, (C, 1), 0).astype(F32)
    rj_f = lax.broadcasted_iota(jnp.int32, (1, C), 1).astype(F32)
    ret_la = -jnp.exp(ret_row[...])
    for h in range(4):
        ci = d * 4 + h
        sl = slice(64 * h, 64 * h + 64)
        la_h = ret_la[:, ci:ci + 1]
        dec = jnp.where(mask, jnp.exp(la_h * adiff), 0.0)
        if d == 0:
            eb_col = jnp.exp(la_h * (ci_f + 1.0))
            kw_row = jnp.exp(la_h * (float(C - 1) - rj_f))
        else:
            eb_col = jnp.exp(la_h * (float(C) - ci_f))
            kw_row = jnp.exp(la_h * rj_f)
        q_h = rq[:, sl]
        v_h = rv[:, sl]
        s_old = rS[0, d, h]
        sc = _dot_nt(q_h, rk[:, sl]) * dec
        o_ref[:, 256 + 64 * h:256 + 64 * h + 64] = _dot(sc, v_h) + eb_col * _dot(q_h, s_old)
        rS[0, d, h] = jnp.exp(la_h * float(C)) * s_old + _dot(rk_t[sl, :] * kw_row, v_h)

    gate = _dot(small, wg[d]) + gb[d]
    lga = (-_softplus(-gate)) * (1.0 / GLA_TAU)
    b = _mask_dot(tri, lga)
    bl_row = b[last:last + 1, :]
    gq = pm[:, GQ:GQ + 128]
    gk = pm[:, GK:GK + 128] * (32.0 ** -0.5)
    gv = pm[:, GV:GV + 256]
    qe = gq * jnp.exp(b)
    ke = gk * jnp.exp(-b)
    kd = gk * jnp.exp(bl_row - b)
    ebl = jnp.exp(bl_row)
    gv_t = gv.T
    for h in range(4):
        sl = slice(32 * h, 32 * h + 32)
        vl = slice(64 * h, 64 * h + 64)
        st_old = gS[0, d, h]
        sc = jnp.where(mask, _dot_nt(qe[:, sl], ke[:, sl]), 0.0)
        o_ref[:, 512 + 64 * h:512 + 64 * h + 64] = _dot(sc, gv[:, vl]) + _dot_nt(qe[:, sl], st_old)
        gS[0, d, h] = st_old * ebl[:, sl] + _dot(gv_t[vl, :], kd[:, sl])

    xr = pm[:, SXBC:SXBC + 512]
    prev = jnp.where(c_idx > 0, hprev[SUBLANES - 1:SUBLANES, :], 0.0)
    nxt = jnp.where(c_idx < n_chunks - 1, hnext[0:1, :], 0.0)
    rowi = lax.broadcasted_iota(jnp.int32, (C, 512), 0)
    x_dn = jnp.where(rowi == 0, prev, pltpu.roll(xr, 1, 0))
    x_up = jnp.where(rowi == C - 1, nxt, pltpu.roll(xr, C - 1, 0))
    y = conv_w[0:1, :] * x_dn + conv_w[1:2, :] * xr + conv_w[2:3, :] * x_up + conv_b[...]
    xbc = y * _sigmoid(y)
    sx = xbc[:, 0:256]
    s_b = xbc[:, 256:384]
    s_c = xbc[:, 384:512]
    sb_t = s_b.T
    raw = [_dot_nt(s_c[:, 64 * g:64 * g + 64], s_b[:, 64 * g:64 * g + 64]) for g in range(2)]
    for h in range(4):
        ci = d * 4 + h
        g = h // 2
        sl = slice(64 * h, 64 * h + 64)
        gl = slice(64 * g, 64 * g + 64)
        b_col = cs_s[d, :, S_DT + ci:S_DT + ci + 1]
        b_row = cst_s[d, S_DT + ci:S_DT + ci + 1, :]
        bl = cst_s[d, S_DT + ci:S_DT + ci + 1, last:last + 1]
        dt_col = sp_s[d, :, S_DT + ci:S_DT + ci + 1]
        dec = jnp.exp(jnp.where(mask, b_col - b_row, neg_inf))
        x_h = sx[:, sl]
        v = x_h * dt_col
        s_old = sS[0, d, h]
        o = _dot(raw[g] * dec, v) + jnp.exp(b_col) * _dot(s_c[:, gl], s_old)
        if d == 0:
            o = o + d_row[:, sl] * x_h
        o_ref[:, 768 + 64 * h:768 + 64 * h + 64] = o
        sS[0, d, h] = jnp.exp(bl) * s_old + _dot(sb_t[gl, :] * jnp.exp(bl - b_row), v)


def _mixer_kernel(n_chunks, use_rope, *refs):
    it = iter(refs)
    pm_f, pm_b, hpf, hnf, hpb, hnb = (next(it) for _ in range(6))
    if use_rope:
        cos_f, sin_f, cos_b, sin_b = (next(it) for _ in range(4))
    else:
        cos_f = sin_f = cos_b = sin_b = None
    mC0, mm0, r0, g0, s0 = (next(it) for _ in range(5))
    small_b, a_row, ret_row, wg, gb, conv_w, conv_b, d_row = (next(it) for _ in range(8))
    o_f, o_b, mC, mm, rS, gS, sS = (next(it) for _ in range(7))
    la_s, lat_s, cs_s, cst_s, sp_s = (next(it) for _ in range(5))

    s = pl.program_id(1)

    @pl.when(s == 0)
    def _():
        mC[...] = mC0[...]
        mm[...] = mm0[...]
        rS[...] = r0[...]
        gS[...] = g0[...]
        sS[...] = s0[...]

    common = (small_b, a_row, ret_row, wg, gb, conv_w, conv_b, d_row)
    scratch = (la_s, lat_s, cs_s, cst_s, sp_s)
    _mixer_direction(0, s, n_chunks, use_rope, pm_f, hpf, hnf, cos_f, sin_f, *common,
                     o_f, mC, mm, rS, gS, sS, *scratch)
    _mixer_direction(1, n_chunks - 1 - s, n_chunks, use_rope, pm_b, hpb, hnb, cos_b, sin_b, *common,
                     o_b, mC, mm, rS, gS, sS, *scratch)


def _mixers(pm, init, params, rope, n_seq, seq_len):
    C = CHUNK
    n = seq_len // C
    t = n_seq * seq_len
    rb = C // SUBLANES
    n_rb = t // SUBLANES
    xbc_blk = SXBC // 512
    use_rope = rope is not None

    def f_idx(b, s):
        return b * n + s

    def b_idx(b, s):
        return b * n + (n - 1 - s)

    in_specs = [
        pl.BlockSpec((C, P_MAIN), lambda b, s: (f_idx(b, s), 0)),
        pl.BlockSpec((C, P_MAIN), lambda b, s: (b_idx(b, s), 0)),
        pl.BlockSpec((SUBLANES, 512), lambda b, s: (jnp.maximum(f_idx(b, s) * rb - 1, 0), xbc_blk)),
        pl.BlockSpec((SUBLANES, 512), lambda b, s: (jnp.minimum(f_idx(b, s) * rb + rb, n_rb - 1), xbc_blk)),
        pl.BlockSpec((SUBLANES, 512), lambda b, s: (jnp.maximum(b_idx(b, s) * rb - 1, 0), xbc_blk)),
        pl.BlockSpec((SUBLANES, 512), lambda b, s: (jnp.minimum(b_idx(b, s) * rb + rb, n_rb - 1), xbc_blk)),
    ]
    args = [pm, pm, pm, pm, pm, pm]
    if use_rope:
        cos, sin = rope
        in_specs += [pl.BlockSpec((C, 256), lambda b, s: (s, 0)),
                     pl.BlockSpec((C, 256), lambda b, s: (s, 0)),
                     pl.BlockSpec((C, 256), lambda b, s: (n - 1 - s, 0)),
                     pl.BlockSpec((C, 256), lambda b, s: (n - 1 - s, 0))]
        args += [cos, sin, cos, sin]

    def state_spec(shape):
        blk = (1,) + tuple(shape[1:])
        nd = len(shape)
        return pl.BlockSpec(blk, lambda b, s: (b,) + (0,) * (nd - 1))

    for a in init:
        in_specs.append(state_spec(a.shape))
        args.append(a)
    for p in params:
        nd = p.ndim
        in_specs.append(pl.BlockSpec(p.shape, lambda b, s, nd=nd: (0,) * nd))
        args.append(p)

    out_specs = [pl.BlockSpec((C, 1024), lambda b, s: (f_idx(b, s), 0)),
                 pl.BlockSpec((C, 1024), lambda b, s: (b_idx(b, s), 0))]
    out_shape = [jax.ShapeDtypeStruct((t, 1024), F32), jax.ShapeDtypeStruct((t, 1024), F32)]
    for a in init:
        out_specs.append(state_spec(a.shape))
        out_shape.append(jax.ShapeDtypeStruct(a.shape, F32))

    scratch = [pltpu.VMEM((2, C, LANES), F32), pltpu.VMEM((2, LANES, C), F32),
               pltpu.VMEM((2, C, LANES), F32), pltpu.VMEM((2, LANES, C), F32),
               pltpu.VMEM((2, C, LANES), F32)]
    return pl.pallas_call(
        functools.partial(_mixer_kernel, n, use_rope),
        grid=(n_seq, n),
        in_specs=in_specs,
        out_specs=out_specs,
        out_shape=out_shape,
        scratch_shapes=scratch,
        compiler_params=pltpu.CompilerParams(dimension_semantics=("arbitrary", "arbitrary"),
                                             vmem_limit_bytes=VMEM_LIMIT),
        name="mixers_rope" if use_rope else "mixers",
    )(*args)


def _post_kernel(of_ref, ob_ref, pg_ref, x_ref, mod_ref, hg_ref, bd_ref, wo_ref, n2_ref,
                 rwh_ref, rwl_ref, rb_ref, x1_ref, h2_ref, lg_ref):
    y = of_ref[...] + ob_ref[...]
    gt = pg_ref[...]
    lane = lax.broadcasted_iota(jnp.int32, (1, y.shape[1]), 1)
    sig = _sigmoid(gt)
    silu = gt * sig
    pre = y * jnp.where(lane >= 768, silu, 1.0)
    ms = jnp.dot((pre * pre).astype(BF16), bd_ref[...], preferred_element_type=F32) * (1.0 / 64.0)
    nrm = pre * lax.rsqrt(ms + EPS) * hg_ref[...]
    post = jnp.where(lane < 256, sig, jnp.where(lane < 768, silu, 1.0))
    mixed = jnp.dot((nrm * post).astype(BF16), wo_ref[...], preferred_element_type=F32)
    x1 = x_ref[...] + mod_ref[0, 2:3, :] * mixed
    x1_ref[...] = x1
    yn = x1 * lax.rsqrt(jnp.mean(x1 * x1, axis=-1, keepdims=True) + EPS)
    h2 = (yn * n2_ref[...]) * (1.0 + mod_ref[0, 4:5, :]) + mod_ref[0, 3:4, :]
    h2_ref[...] = h2.astype(BF16)
    lg_ref[...] = _dot_hi(h2, rwh_ref[...], rwl_ref[...]) + rb_ref[...]


def _post(o_f, o_b, pg, x, mod, hg, bd, w_out, n2g, rw_hi, rw_lo, rb, seq_len, tm):
    t, d = x.shape
    tiles_per_seq = seq_len // tm
    n_mod = mod.shape[0]
    mod_map = (lambda i: (i // tiles_per_seq, 0, 0)) if n_mod > 1 else (lambda i: (0, 0, 0))
    row = lambda i: (i, 0)
    const = lambda i: (0, 0)
    return pl.pallas_call(
        _post_kernel,
        grid=(t // tm,),
        in_specs=[pl.BlockSpec((tm, d), row), pl.BlockSpec((tm, d), row), pl.BlockSpec((tm, d), row),
                  pl.BlockSpec((tm, d), row), pl.BlockSpec((1, 8, d), mod_map),
                  pl.BlockSpec((1, d), const), pl.BlockSpec((d, d), const), pl.BlockSpec((d, d), const),
                  pl.BlockSpec((1, d), const), pl.BlockSpec((d, LANES), const),
                  pl.BlockSpec((d, LANES), const), pl.BlockSpec((1, LANES), const)],
        out_specs=[pl.BlockSpec((tm, d), row), pl.BlockSpec((tm, d), row), pl.BlockSpec((tm, LANES), row)],
        out_shape=[jax.ShapeDtypeStruct((t, d), F32), jax.ShapeDtypeStruct((t, d), BF16),
                   jax.ShapeDtypeStruct((t, LANES), F32)],
        compiler_params=pltpu.CompilerParams(dimension_semantics=("arbitrary",),
                                             vmem_limit_bytes=VMEM_LIMIT),
        name="post_mix",
    )(o_f, o_b, pg, x, mod, hg, bd, w_out, n2g, rw_hi, rw_lo, rb)


def _moe_kernel(be_ref, nu_ref, xb_ref, wgu_ref, bgu_ref, wdn_ref, bdn_ref, o_ref):
    i = pl.program_id(0)
    d_ff = wdn_ref.shape[1]

    @pl.when(i < nu_ref[0])
    def _():
        gu = jnp.dot(xb_ref[...], wgu_ref[0], preferred_element_type=F32) + bgu_ref[0]
        gate = jnp.minimum(gu[:, :d_ff], SWIGLU_LIMIT)
        up = jnp.clip(gu[:, d_ff:], -SWIGLU_LIMIT, SWIGLU_LIMIT)
        act = gate * _sigmoid(SWIGLU_ALPHA * gate) * (up + 1.0)
        o_ref[...] = jnp.dot(act.astype(BF16), wdn_ref[0], preferred_element_type=F32) + bdn_ref[0]

    @pl.when(i >= nu_ref[0])
    def _():
        o_ref[...] = jnp.zeros_like(o_ref)


def _moe(xb, block_e, n_used, w_gu, b_gu, w_dn, b_dn, bm):
    rows, d = xb.shape
    n_e, _, n_gu = w_gu.shape
    d_ff = w_dn.shape[1]
    grid_spec = pltpu.PrefetchScalarGridSpec(
        num_scalar_prefetch=2,
        grid=(rows // bm,),
        in_specs=[pl.BlockSpec((bm, d), lambda i, be, nu: (i, 0)),
                  pl.BlockSpec((1, d, n_gu), lambda i, be, nu: (be[i], 0, 0)),
                  pl.BlockSpec((1, 1, n_gu), lambda i, be, nu: (be[i], 0, 0)),
                  pl.BlockSpec((1, d_ff, d), lambda i, be, nu: (be[i], 0, 0)),
                  pl.BlockSpec((1, 1, d), lambda i, be, nu: (be[i], 0, 0))],
        out_specs=pl.BlockSpec((bm, d), lambda i, be, nu: (i, 0)),
    )
    return pl.pallas_call(
        _moe_kernel,
        grid_spec=grid_spec,
        out_shape=jax.ShapeDtypeStruct((rows, d), F32),
        compiler_params=pltpu.CompilerParams(dimension_semantics=("arbitrary",),
                                             vmem_limit_bytes=VMEM_LIMIT),
        name="moe_ffn",
    )(block_e, n_used, xb, w_gu, b_gu.reshape(n_e, 1, n_gu), w_dn, b_dn.reshape(n_e, 1, d))


def _combine_kernel(final, x1_ref, y4_ref, gt_ref, mod_ref, fg_ref, o_ref):
    d = x1_ref.shape[1]
    gt = gt_ref[...]
    y = gt[:, 0:1] * y4_ref[:, 0:d]
    for k in range(1, TOP_K):
        y = y + gt[:, k:k + 1] * y4_ref[:, k * d:(k + 1) * d]
    x2 = x1_ref[...] + mod_ref[0, 5:6, :] * y
    if final:
        x2 = x2 * lax.rsqrt(jnp.mean(x2 * x2, axis=-1, keepdims=True) + EPS) * fg_ref[...]
    o_ref[...] = x2


def _combine(x1, y4, gates, mod, final_g, final, seq_len, tm):
    t, d = x1.shape
    tiles_per_seq = seq_len // tm
    n_mod = mod.shape[0]
    mod_map = (lambda i: (i // tiles_per_seq, 0, 0)) if n_mod > 1 else (lambda i: (0, 0, 0))
    return pl.pallas_call(
        functools.partial(_combine_kernel, final),
        grid=(t // tm,),
        in_specs=[pl.BlockSpec((tm, d), lambda i: (i, 0)),
                  pl.BlockSpec((tm, TOP_K * d), lambda i: (i, 0)),
                  pl.BlockSpec((tm, TOP_K), lambda i: (i, 0)),
                  pl.BlockSpec((1, 8, d), mod_map),
                  pl.BlockSpec((1, d), lambda i: (0, 0))],
        out_specs=pl.BlockSpec((tm, d), lambda i: (i, 0)),
        out_shape=jax.ShapeDtypeStruct((t, d), F32),
        compiler_params=pltpu.CompilerParams(dimension_semantics=("arbitrary",),
                                             vmem_limit_bytes=VMEM_LIMIT),
        name="moe_combine",
    )(x1, y4, gates, mod, final_g)


def _rope_tables(n_tokens):
    rows = n_tokens // GRID_W
    r, c = jnp.meshgrid(jnp.arange(rows), jnp.arange(GRID_W), indexing='ij')
    nf = 16
    inv = ROPE_BASE ** (-jnp.arange(nf, dtype=F32) / nf)
    ang = jnp.concatenate([r.reshape(-1, 1).astype(F32) * inv, c.reshape(-1, 1).astype(F32) * inv], axis=-1)
    cos, sin = jnp.cos(ang), jnp.sin(ang)
    cos_t = jnp.tile(jnp.concatenate([cos, cos], axis=-1), (1, 4))
    sin_t = jnp.tile(jnp.concatenate([-sin, sin], axis=-1), (1, 4))
    return cos_t, sin_t


def _pad_lanes(a, width=LANES):
    return jnp.pad(a, [(0, 0)] * (a.ndim - 1) + [(0, width - a.shape[-1])])


def kernel(x_prompt, x_sample, c, state_mlstm_C, state_mlstm_n, state_mlstm_m, state_ret, state_gla, state_ssd,
           c_ctx, ada_w, ada_b, norm1_g, norm2_g, final_g, w_in, w_out,
           mlstm_i_b, mlstm_f_b, mlstm_norm_g, ret_decay, ret_norm_g,
           gla_gate_w, gla_gate_b, gla_norm_g,
           ssd_conv_w, ssd_conv_b, ssd_dt_b, ssd_A_log, ssd_D, ssd_norm_g,
           router_w, router_b, moe_w_gu, moe_b_gu, moe_w_down, moe_b_down):
    depth = w_in.shape[0]
    bp, lp, d = x_prompt.shape
    bs, ls, _ = x_sample.shape
    n_e = router_w.shape[-1]
    tm = 256
    bm = 256

    offs = np.cumsum([0, 256, 256, 256, 256, 8, 8, 256, 256, 256, 256, 128, 128, 256, 256, 32, 256, 512, 8])
    (o_mq, o_mk, o_mv, o_mo, o_mi, o_mf, o_rq, o_rk, o_rv, o_rg, o_gq, o_gk, o_gv, o_gr, o_glr,
     o_sz, o_sxbc, o_sdt, _) = [int(v) for v in offs]

    def cols(a, n):
        return w_in[:, :, a:a + n]

    small_w = jnp.concatenate([cols(o_mi, 8), cols(o_mf, 8), cols(o_sdt, 8),
                               jnp.zeros((depth, d, 8), F32), cols(o_glr, 32),
                               jnp.zeros((depth, d, LANES - 64), F32)], axis=-1)
    w_main = jnp.concatenate([cols(o_mq, 256), cols(o_mk, 256), cols(o_mv, 256),
                              cols(o_rq, 256), cols(o_rk, 256), cols(o_rv, 256),
                              cols(o_gq, 128), cols(o_gk, 128), cols(o_gv, 256),
                              cols(o_sxbc, 512), small_w], axis=-1).astype(BF16)
    w_gate = jnp.concatenate([cols(o_mo, 256), cols(o_rg, 256), cols(o_gr, 256), cols(o_sz, 256)],
                             axis=-1).astype(BF16)
    w_out_b = w_out.astype(BF16)
    w_gu_b = moe_w_gu.astype(BF16)
    w_dn_b = moe_w_down.astype(BF16)
    rw = _pad_lanes(router_w)
    rw_hi = rw.astype(BF16)
    rw_lo = (rw - rw_hi.astype(F32)).astype(BF16)
    rb = _pad_lanes(router_b)[:, None, :]
    hg = jnp.concatenate([mlstm_norm_g, ret_norm_g, gla_norm_g, ssd_norm_g], axis=-1)[:, None, :]
    gidx = np.arange(d) // 64
    bd = jnp.asarray(gidx[:, None] == gidx[None, :], dtype=BF16)

    small_b = _pad_lanes(jnp.concatenate([mlstm_i_b.reshape(depth, 8), mlstm_f_b.reshape(depth, 8),
                                          ssd_dt_b.reshape(depth, 8)], axis=-1))[:, None, :]
    a_row = _pad_lanes(jnp.concatenate([jnp.zeros((depth, 16), F32), ssd_A_log.reshape(depth, 8)],
                                       axis=-1))[:, None, :]
    ret_row = _pad_lanes(ret_decay.reshape(depth, 8))[:, None, :]
    wg = jnp.zeros((depth, 2, LANES, LANES), F32)
    wg = wg.at[:, 0, S_GLR:S_GLR + 16, :].set(gla_gate_w[:, 0])
    wg = wg.at[:, 1, S_GLR + 16:S_GLR + 32, :].set(gla_gate_w[:, 1])
    wg = wg.astype(BF16)
    gb = gla_gate_b[:, :, None, :]
    d_row = jnp.repeat(ssd_D, 64, axis=-1)[:, None, :]
    conv_b = ssd_conv_b[:, None, :]

    n_cond = 1 + bs
    pad_rows = (-n_cond) % 8
    conds = jnp.concatenate([c_ctx[None, :], c, jnp.zeros((pad_rows, d), F32)], axis=0)
    mod_all = _ada_mod(conds, ada_w, ada_b).reshape(depth, n_cond + pad_rows, 6, d)
    mod_all = jnp.pad(mod_all, ((0, 0), (0, 0), (0, 2), (0, 0)))

    rope = _rope_tables(ls)

    def aug(cm, nm):
        return jnp.concatenate([cm, nm[..., None], jnp.zeros(cm.shape[:-1] + (63,), F32)], axis=-1)

    def m_rows(mv):
        return jnp.broadcast_to(mv.reshape(mv.shape[0], 8, 1), (mv.shape[0], 8, LANES))

    zero = lambda *s: jnp.zeros(s, F32)
    init_ctx = (zero(bp, 2, 4, 64, 128), zero(bp, 8, LANES), zero(bp, 2, 4, 64, 64),
                zero(bp, 2, 4, 64, 32), zero(bp, 2, 4, 64, 64))

    streams = [dict(x=x_prompt.reshape(bp * lp, d), n_seq=bp, seq_len=lp, rope=None, mrow=slice(0, 1)),
               dict(x=x_sample.reshape(bs * ls, d), n_seq=bs, seq_len=ls, rope=rope, mrow=slice(1, 1 + bs))]
    t_tot = bp * lp + bs * ls
    n_assign = t_tot * TOP_K
    n_blocks = n_assign // bm + n_e
    rows = n_blocks * bm
    finals = []
    out_final = [None, None]
    for l in range(depth):
        params = (small_b[l], a_row[l], ret_row[l], wg[l], gb[l], ssd_conv_w[l], conv_b[l], d_row[l])
        fin_l = None
        for si, st in enumerate(streams):
            mod = mod_all[l, st['mrow']]
            if si == 0:
                init = init_ctx
            else:
                init = (aug(state_mlstm_C[:, l], state_mlstm_n[:, l]), m_rows(state_mlstm_m[:, l]),
                        state_ret[:, l], jnp.swapaxes(state_gla[:, l], -1, -2), state_ssd[:, l])
            pm, pg = _proj(st['x'], mod, norm1_g[l][None, :], w_main[l], w_gate[l], st['seq_len'], tm)
            res = _mixers(pm, init, params, st['rope'], st['n_seq'], st['seq_len'])
            o_f, o_b = res[0], res[1]
            if si == 0:
                fin_l = res[2:]
            st['x1'], st['h2'], st['lg'] = _post(o_f, o_b, pg, st['x'], mod, hg[l], bd, w_out_b[l],
                                                 norm2_g[l][None, :], rw_hi[l], rw_lo[l], rb[l],
                                                 st['seq_len'], tm)
        finals.append(fin_l)

        h2 = jnp.concatenate([st['h2'] for st in streams], axis=0)
        logits = jnp.concatenate([st['lg'] for st in streams], axis=0)[:, :n_e]
        top_val, top_idx = lax.top_k(logits, TOP_K)
        gates = jax.nn.softmax(top_val, axis=-1)
        flat_e = top_idx.reshape(-1)
        onehot = (flat_e[:, None] == jnp.arange(n_e, dtype=flat_e.dtype)[None, :]).astype(jnp.int32)
        csum = jnp.cumsum(onehot, axis=0)
        rank = jnp.sum((csum - onehot) * onehot, axis=1)
        counts = csum[-1]
        padded = (counts + bm - 1) // bm * bm
        pad_end = jnp.cumsum(padded)
        pad_start = pad_end - padded
        dest = (pad_start[flat_e] + rank).astype(jnp.int32)
        tok = (jnp.arange(n_assign, dtype=jnp.int32) // TOP_K)
        src = jnp.full((rows,), t_tot, dtype=jnp.int32).at[dest].set(tok)
        block_e = jnp.minimum(jnp.searchsorted(pad_end, jnp.arange(n_blocks, dtype=pad_end.dtype) * bm,
                                               side='right'), n_e - 1).astype(jnp.int32)
        n_used = (pad_end[-1] // bm).astype(jnp.int32).reshape(1)
        h2_pad = jnp.concatenate([h2, jnp.zeros((1, d), h2.dtype)], axis=0)
        xb = h2_pad[src]
        yb = _moe(xb, block_e, n_used, w_gu_b[l], moe_b_gu[l], w_dn_b[l], moe_b_down[l], bm)
        y4 = yb[dest].reshape(t_tot, TOP_K * d)

        off = 0
        for si, st in enumerate(streams):
            nt = st['n_seq'] * st['seq_len']
            mod = mod_all[l, st['mrow']]
            st['x'] = _combine(st['x1'], y4[off:off + nt], gates[off:off + nt], mod, final_g[None, :],
                               l == depth - 1, st['seq_len'], tm)
            off += nt

    y_prompt = streams[0]['x'].reshape(bp, lp, d)
    y_sample = streams[1]['x'].reshape(bs, ls, d)
    new_c = jnp.stack([f[0][..., :64] for f in finals], axis=1)
    new_n = jnp.stack([f[0][..., 64] for f in finals], axis=1)
    new_m = jnp.stack([f[1][:, :, 0].reshape(bp, 2, 4) for f in finals], axis=1)
    new_ret = jnp.stack([f[2] for f in finals], axis=1)
    new_gla = jnp.stack([jnp.swapaxes(f[3], -1, -2) for f in finals], axis=1)
    new_ssd = jnp.stack([f[4] for f in finals], axis=1)
    return (y_prompt, y_sample, new_c, new_n, new_m, new_ret, new_gla, new_ssd)
```

```python
import functools

import numpy as np
import jax
import jax.numpy as jnp
from jax import lax
from jax.experimental import pallas as pl
from jax.experimental.pallas import tpu as pltpu

F32 = jnp.float32
BF16 = jnp.bfloat16

EPS = 1e-6
CHUNK = 64
GRID_W = 64
ROPE_BASE = 10000.0
GLA_TAU = 16.0
TOP_K = 4
SWIGLU_LIMIT = 7.0
SWIGLU_ALPHA = 1.702
LANES = 128
SUBLANES = 8
VMEM_LIMIT = 56 * 1024 * 1024

MQ, MK, MV = 0, 256, 512
RQ, RK, RV = 768, 1024, 1280
GQ, GK, GV = 1536, 1664, 1792
SXBC = 2048
SMALL = 2560
P_MAIN = 2688
S_MI, S_MF, S_DT, S_DTRAW, S_GLR = 0, 8, 16, 24, 32


def _softplus(x):
    return jnp.maximum(x, 0.0) + jnp.log1p(jnp.exp(-jnp.abs(x)))


def _sigmoid(x):
    return 1.0 / (1.0 + jnp.exp(-x))


def _dot(a, b):
    return jnp.dot(a.astype(BF16), b.astype(BF16), preferred_element_type=F32)


def _dot_nt(a, b):
    return lax.dot_general(a.astype(BF16), b.astype(BF16), (((1,), (1,)), ((), ())),
                           preferred_element_type=F32)


def _split3(x):
    x1 = x.astype(BF16)
    r1 = x - x1.astype(F32)
    x2 = r1.astype(BF16)
    r2 = r1 - x2.astype(F32)
    return x1, x2, r2.astype(BF16)


def _mask_dot(mask_bf16, x):
    x1, x2, x3 = _split3(x)
    d = functools.partial(jnp.dot, preferred_element_type=F32)
    return (d(mask_bf16, x3) + d(mask_bf16, x2)) + d(mask_bf16, x1)


def _dot_mask(x, mask_bf16):
    x1, x2, x3 = _split3(x)
    d = functools.partial(jnp.dot, preferred_element_type=F32)
    return (d(x3, mask_bf16) + d(x2, mask_bf16)) + d(x1, mask_bf16)


def _dot_hi(a, b_hi, b_lo):
    a1 = a.astype(BF16)
    a2 = (a - a1.astype(F32)).astype(BF16)
    d = functools.partial(jnp.dot, preferred_element_type=F32)
    return (d(a1, b_lo) + d(a2, b_hi)) + d(a1, b_hi)


def _ada_kernel(c_ref, w_ref, b_ref, o_ref):
    c = c_ref[...]
    s = c * _sigmoid(c)
    w = w_ref[0]
    w_hi = w.astype(BF16)
    w_lo = (w - w_hi.astype(F32)).astype(BF16)
    o_ref[0] = _dot_hi(s, w_hi, w_lo) + b_ref[0]


def _ada_mod(conds, ada_w, ada_b):
    depth, d, n6 = ada_w.shape
    rows = conds.shape[0]
    tn = 1536
    return pl.pallas_call(
        _ada_kernel,
        grid=(depth, n6 // tn),
        in_specs=[pl.BlockSpec((rows, d), lambda l, j: (0, 0)),
                  pl.BlockSpec((1, d, tn), lambda l, j: (l, 0, j)),
                  pl.BlockSpec((1, 1, tn), lambda l, j: (l, 0, j))],
        out_specs=pl.BlockSpec((1, rows, tn), lambda l, j: (l, 0, j)),
        out_shape=jax.ShapeDtypeStruct((depth, rows, n6), F32),
        compiler_params=pltpu.CompilerParams(dimension_semantics=("arbitrary", "arbitrary"),
                                             vmem_limit_bytes=VMEM_LIMIT),
        name="ada_mod",
    )(conds, ada_w, ada_b.reshape(depth, 1, n6))


def _proj_kernel(x_ref, mod_ref, g_ref, wm_ref, wg_ref, pm_ref, pg_ref):
    x = x_ref[...]
    y = x * lax.rsqrt(jnp.mean(x * x, axis=-1, keepdims=True) + EPS)
    h = (y * g_ref[...]) * (1.0 + mod_ref[0, 1:2, :]) + mod_ref[0, 0:1, :]
    hb = h.astype(BF16)
    pm_ref[...] = jnp.dot(hb, wm_ref[...], preferred_element_type=F32)
    pg_ref[...] = jnp.dot(hb, wg_ref[...], preferred_element_type=F32)


def _mod_map(n_ctx_tiles, tiles_per_latent_seq):
    def index(i, *_):
        return (jnp.where(i < n_ctx_tiles, 0, 1 + (i - n_ctx_tiles) // tiles_per_latent_seq), 0, 0)
    return index


def _proj(x, mod, g, w_main, w_gate, mod_map, tm):
    t, d = x.shape
    return pl.pallas_call(
        _proj_kernel,
        grid=(t // tm,),
        in_specs=[pl.BlockSpec((tm, d), lambda i: (i, 0)),
                  pl.BlockSpec((1, 8, d), mod_map),
                  pl.BlockSpec((1, d), lambda i: (0, 0)),
                  pl.BlockSpec(w_main.shape, lambda i: (0, 0)),
                  pl.BlockSpec(w_gate.shape, lambda i: (0, 0))],
        out_specs=[pl.BlockSpec((tm, w_main.shape[1]), lambda i: (i, 0)),
                   pl.BlockSpec((tm, w_gate.shape[1]), lambda i: (i, 0))],
        out_shape=[jax.ShapeDtypeStruct((t, w_main.shape[1]), F32),
                   jax.ShapeDtypeStruct((t, w_gate.shape[1]), F32)],
        compiler_params=pltpu.CompilerParams(dimension_semantics=("arbitrary",),
                                             vmem_limit_bytes=VMEM_LIMIT),
        name="in_proj",
    )(x, mod, g, w_main, w_gate)


def _mixer_direction(d, c_idx, n_chunks, use_rope, pm, hprev, hnext, cos_ref, sin_ref,
                     small_b, a_row, ret_row, wg, gb, conv_w, conv_b, d_row,
                     o_ref, mC, mm, rS, gS, sS, la_s, lat_s, cs_s, cst_s, sp_s):
    C = CHUNK
    last = C - 1 if d == 0 else 0
    row = lax.broadcasted_iota(jnp.int32, (C, C), 0)
    col = lax.broadcasted_iota(jnp.int32, (C, C), 1)
    mask = (col <= row) if d == 0 else (col >= row)
    mask_t = (row <= col) if d == 0 else (row >= col)
    tri = mask.astype(BF16)
    tri_t = mask_t.astype(BF16)
    neg_inf = jnp.float32(-jnp.inf)

    small = pm[:, SMALL:SMALL + LANES]
    pre = small + small_b[...]
    lane = lax.broadcasted_iota(jnp.int32, (1, LANES), 1)
    sp = _softplus(pre)
    lsg = -_softplus(-pre)
    a_neg = -jnp.exp(a_row[...])
    la = jnp.where(lane < S_MF, pre,
                   jnp.where(lane < S_DT, lsg,
                             jnp.where(lane < S_DTRAW, sp * a_neg, 0.0)))
    la_t = la.T
    la_s[d] = la
    lat_s[d] = la_t
    cs_s[d] = _mask_dot(tri, la)
    cst_s[d] = _dot_mask(la_t, tri_t)
    sp_s[d] = sp

    q_all = pm[:, MQ:MQ + 256] * (64.0 ** -0.5)
    k_all = pm[:, MK:MK + 256]
    v_all = pm[:, MV:MV + 256]
    k_t = k_all.T
    ones_col = (lax.broadcasted_iota(jnp.int32, (C, 64), 1) == 0).astype(F32)
    for h in range(4):
        ci = d * 4 + h
        sl = slice(64 * h, 64 * h + 64)
        b_col = cs_s[d, :, S_MF + ci:S_MF + ci + 1]
        b_row = cst_s[d, S_MF + ci:S_MF + ci + 1, :]
        bl = cst_s[d, S_MF + ci:S_MF + ci + 1, last:last + 1]
        u_row = lat_s[d, S_MI + ci:S_MI + ci + 1, :] - b_row
        ms = mm[0, ci:ci + 1, 0:1]
        d_log = jnp.where(mask, b_col + u_row, neg_inf)
        inter = b_col + ms
        m_t = jnp.maximum(inter, jnp.max(d_log, axis=1, keepdims=True))
        w = jnp.exp(d_log - m_t)
        q_h = q_all[:, sl]
        sc = _dot_nt(q_h, k_all[:, sl]) * w
        w_inter = jnp.exp(inter - m_t)
        v_aug = jnp.concatenate([v_all[:, sl], ones_col], axis=1)
        c_aug = mC[0, d, h]
        num = _dot(sc, v_aug) + w_inter * _dot(q_h, c_aug)
        den = num[:, 64:65]
        o_ref[:, sl] = num[:, 0:64] / jnp.maximum(jnp.abs(den), jnp.exp(-m_t))
        g_row = bl + u_row
        m_new = jnp.maximum(bl + ms, jnp.max(g_row, axis=1, keepdims=True))
        wk_row = jnp.exp(g_row - m_new)
        decay = jnp.exp(bl + ms - m_new)
        mC[0, d, h] = decay * c_aug + _dot(k_t[sl, :] * wk_row, v_aug)
        mm[0, ci:ci + 1, :] = jnp.broadcast_to(m_new, (1, LANES))

    rq = pm[:, RQ:RQ + 256]
    rk = pm[:, RK:RK + 256] * (64.0 ** -0.5)
    rv = pm[:, RV:RV + 256]
    if use_rope:
        cosv = cos_ref[...]
        sinv = sin_ref[...]
        first = (lax.broadcasted_iota(jnp.int32, (1, 256), 1) % 64) < 32

        def rope(x):
            rot = jnp.where(first, pltpu.roll(x, 256 - 32, 1), pltpu.roll(x, 32, 1))
            return x * cosv + rot * sinv

        rq = rope(rq)
        rk = rope(rk)
    rk_t = rk.T
    adiff = jnp.abs(row - col).astype(F32)
    ci_f = lax.broadcasted_iota(jnp.int32, (C, 1), 0).astype(F32)
    rj_f = lax.broadcasted_iota(jnp.int32, (1, C), 1).astype(F32)
    ret_la = -jnp.exp(ret_row[...])
    for h in range(4):
        ci = d * 4 + h
        sl = slice(64 * h, 64 * h + 64)
        la_h = ret_la[:, ci:ci + 1]
        dec = jnp.where(mask, jnp.exp(la_h * adiff), 0.0)
        if d == 0:
            eb_col = jnp.exp(la_h * (ci_f + 1.0))
            kw_row = jnp.exp(la_h * (float(C - 1) - rj_f))
        else:
            eb_col = jnp.exp(la_h * (float(C) - ci_f))
            kw_row = jnp.exp(la_h * rj_f)
        q_h = rq[:, sl]
        v_h = rv[:, sl]
        s_old = rS[0, d, h]
        sc = _dot_nt(q_h, rk[:, sl]) * dec
        o_ref[:, 256 + 64 * h:256 + 64 * h + 64] = _dot(sc, v_h) + eb_col * _dot(q_h, s_old)
        rS[0, d, h] = jnp.exp(la_h * float(C)) * s_old + _dot(rk_t[sl, :] * kw_row, v_h)

    gate = _dot(small, wg[d]) + gb[d]
    lga = (-_softplus(-gate)) * (1.0 / GLA_TAU)
    b = _mask_dot(tri, lga)
    bl_row = b[last:last + 1, :]
    gq = pm[:, GQ:GQ + 128]
    gk = pm[:, GK:GK + 128] * (32.0 ** -0.5)
    gv = pm[:, GV:GV + 256]
    qe = gq * jnp.exp(b)
    ke = gk * jnp.exp(-b)
    kd = gk * jnp.exp(bl_row - b)
    ebl = jnp.exp(bl_row)
    gv_t = gv.T
    for h in range(4):
        sl = slice(32 * h, 32 * h + 32)
        vl = slice(64 * h, 64 * h + 64)
        st_old = gS[0, d, h]
        sc = jnp.where(mask, _dot_nt(qe[:, sl], ke[:, sl]), 0.0)
        o_ref[:, 512 + 64 * h:512 + 64 * h + 64] = _dot(sc, gv[:, vl]) + _dot_nt(qe[:, sl], st_old)
        gS[0, d, h] = st_old * ebl[:, sl] + _dot(gv_t[vl, :], kd[:, sl])

    xr = pm[:, SXBC:SXBC + 512]
    prev = jnp.where(c_idx > 0, hprev[SUBLANES - 1:SUBLANES, :], 0.0)
    nxt = jnp.where(c_idx < n_chunks - 1, hnext[0:1, :], 0.0)
    rowi = lax.broadcasted_iota(jnp.int32, (C, 512), 0)
    x_dn = jnp.where(rowi == 0, prev, pltpu.roll(xr, 1, 0))
    x_up = jnp.where(rowi == C - 1, nxt, pltpu.roll(xr, C - 1, 0))
    y = conv_w[0:1, :] * x_dn + conv_w[1:2, :] * xr + conv_w[2:3, :] * x_up + conv_b[...]
    xbc = y * _sigmoid(y)
    sx = xbc[:, 0:256]
    s_b = xbc[:, 256:384]
    s_c = xbc[:, 384:512]
    sb_t = s_b.T
    raw = [_dot_nt(s_c[:, 64 * g:64 * g + 64], s_b[:, 64 * g:64 * g + 64]) for g in range(2)]
    for h in range(4):
        ci = d * 4 + h
        g = h // 2
        sl = slice(64 * h, 64 * h + 64)
        gl = slice(64 * g, 64 * g + 64)
        b_col = cs_s[d, :, S_DT + ci:S_DT + ci + 1]
        b_row = cst_s[d, S_DT + ci:S_DT + ci + 1, :]
        bl = cst_s[d, S_DT + ci:S_DT + ci + 1, last:last + 1]
        dt_col = sp_s[d, :, S_DT + ci:S_DT + ci + 1]
        dec = jnp.exp(jnp.where(mask, b_col - b_row, neg_inf))
        x_h = sx[:, sl]
        v = x_h * dt_col
        s_old = sS[0, d, h]
        o = _dot(raw[g] * dec, v) + jnp.exp(b_col) * _dot(s_c[:, gl], s_old)
        if d == 0:
            o = o + d_row[:, sl] * x_h
        o_ref[:, 768 + 64 * h:768 + 64 * h + 64] = o
        sS[0, d, h] = jnp.exp(bl) * s_old + _dot(sb_t[gl, :] * jnp.exp(bl - b_row), v)


def _mixer_kernel(n_chunks, use_rope, *refs):
    it = iter(refs)
    pm_f, pm_b, hpf, hnf, hpb, hnb = (next(it) for _ in range(6))
    if use_rope:
        cos_f, sin_f, cos_b, sin_b = (next(it) for _ in range(4))
    else:
        cos_f = sin_f = cos_b = sin_b = None
    mC0, mm0, r0, g0, s0 = (next(it) for _ in range(5))
    small_b, a_row, ret_row, wg, gb, conv_w, conv_b, d_row = (next(it) for _ in range(8))
    o_f, o_b, mC, mm, rS, gS, sS = (next(it) for _ in range(7))
    la_s, lat_s, cs_s, cst_s, sp_s = (next(it) for _ in range(5))

    s = pl.program_id(1)

    @pl.when(s == 0)
    def _():
        mC[...] = mC0[...]
        mm[...] = mm0[...]
        rS[...] = r0[...]
        gS[...] = g0[...]
        sS[...] = s0[...]

    common = (small_b, a_row, ret_row, wg, gb, conv_w, conv_b, d_row)
    scratch = (la_s, lat_s, cs_s, cst_s, sp_s)
    _mixer_direction(0, s, n_chunks, use_rope, pm_f, hpf, hnf, cos_f, sin_f, *common,
                     o_f, mC, mm, rS, gS, sS, *scratch)
    _mixer_direction(1, n_chunks - 1 - s, n_chunks, use_rope, pm_b, hpb, hnb, cos_b, sin_b, *common,
                     o_b, mC, mm, rS, gS, sS, *scratch)


def _mixers(pm, init, params, rope, n_seq, seq_len, row0):
    C = CHUNK
    n = seq_len // C
    t = pm.shape[0]
    chunk0 = row0 // C
    rb = C // SUBLANES
    n_rb = t // SUBLANES
    xbc_blk = SXBC // 512
    use_rope = rope is not None

    def f_idx(b, s):
        return chunk0 + b * n + s

    def b_idx(b, s):
        return chunk0 + b * n + (n - 1 - s)

    in_specs = [
        pl.BlockSpec((C, P_MAIN), lambda b, s: (f_idx(b, s), 0)),
        pl.BlockSpec((C, P_MAIN), lambda b, s: (b_idx(b, s), 0)),
        pl.BlockSpec((SUBLANES, 512), lambda b, s: (jnp.maximum(f_idx(b, s) * rb - 1, 0), xbc_blk)),
        pl.BlockSpec((SUBLANES, 512), lambda b, s: (jnp.minimum(f_idx(b, s) * rb + rb, n_rb - 1), xbc_blk)),
        pl.BlockSpec((SUBLANES, 512), lambda b, s: (jnp.maximum(b_idx(b, s) * rb - 1, 0), xbc_blk)),
        pl.BlockSpec((SUBLANES, 512), lambda b, s: (jnp.minimum(b_idx(b, s) * rb + rb, n_rb - 1), xbc_blk)),
    ]
    args = [pm, pm, pm, pm, pm, pm]
    if use_rope:
        cos, sin = rope
        in_specs += [pl.BlockSpec((C, 256), lambda b, s: (s, 0)),
                     pl.BlockSpec((C, 256), lambda b, s: (s, 0)),
                     pl.BlockSpec((C, 256), lambda b, s: (n - 1 - s, 0)),
                     pl.BlockSpec((C, 256), lambda b, s: (n - 1 - s, 0))]
        args += [cos, sin, cos, sin]

    def state_spec(shape):
        blk = (1,) + tuple(shape[1:])
        nd = len(shape)
        return pl.BlockSpec(blk, lambda b, s: (b,) + (0,) * (nd - 1))

    for a in init:
        in_specs.append(state_spec(a.shape))
        args.append(a)
    for p in params:
        nd = p.ndim
        in_specs.append(pl.BlockSpec(p.shape, lambda b, s, nd=nd: (0,) * nd))
        args.append(p)

    t_out = n_seq * seq_len
    out_specs = [pl.BlockSpec((C, 1024), lambda b, s: (b * n + s, 0)),
                 pl.BlockSpec((C, 1024), lambda b, s: (b * n + (n - 1 - s), 0))]
    out_shape = [jax.ShapeDtypeStruct((t_out, 1024), F32), jax.ShapeDtypeStruct((t_out, 1024), F32)]
    for a in init:
        out_specs.append(state_spec(a.shape))
        out_shape.append(jax.ShapeDtypeStruct(a.shape, F32))

    scratch = [pltpu.VMEM((2, C, LANES), F32), pltpu.VMEM((2, LANES, C), F32),
               pltpu.VMEM((2, C, LANES), F32), pltpu.VMEM((2, LANES, C), F32),
               pltpu.VMEM((2, C, LANES), F32)]
    return pl.pallas_call(
        functools.partial(_mixer_kernel, n, use_rope),
        grid=(n_seq, n),
        in_specs=in_specs,
        out_specs=out_specs,
        out_shape=out_shape,
        scratch_shapes=scratch,
        compiler_params=pltpu.CompilerParams(dimension_semantics=("arbitrary", "arbitrary"),
                                             vmem_limit_bytes=VMEM_LIMIT),
        name="mixers_rope" if use_rope else "mixers",
    )(*args)


def _post_kernel(n_experts, n_ctx_tiles, ofc_ref, obc_ref, ofl_ref, obl_ref, pg_ref, x_ref, mod_ref,
                 hg_ref, bd_ref, wo_ref, n2_ref, rwh_ref, rwl_ref, rb_ref, ltri_ref,
                 x1_ref, h2_ref, ei_ref, gt_ref, rk_ref, cnt_ref):
    @pl.when(pl.program_id(0) == 0)
    def _():
        cnt_ref[...] = jnp.zeros_like(cnt_ref)

    y = jnp.where(pl.program_id(0) < n_ctx_tiles, ofc_ref[...] + obc_ref[...], ofl_ref[...] + obl_ref[...])
    gt = pg_ref[...]
    lane = lax.broadcasted_iota(jnp.int32, (1, y.shape[1]), 1)
    sig = _sigmoid(gt)
    silu = gt * sig
    pre = y * jnp.where(lane >= 768, silu, 1.0)
    ms = jnp.dot((pre * pre).astype(BF16), bd_ref[...], preferred_element_type=F32) * (1.0 / 64.0)
    nrm = pre * lax.rsqrt(ms + EPS) * hg_ref[...]
    post = jnp.where(lane < 256, sig, jnp.where(lane < 768, silu, 1.0))
    mixed = jnp.dot((nrm * post).astype(BF16), wo_ref[...], preferred_element_type=F32)
    x1 = x_ref[...] + mod_ref[0, 2:3, :] * mixed
    x1_ref[...] = x1
    yn = x1 * lax.rsqrt(jnp.mean(x1 * x1, axis=-1, keepdims=True) + EPS)
    h2 = (yn * n2_ref[...]) * (1.0 + mod_ref[0, 4:5, :]) + mod_ref[0, 3:4, :]
    h2_ref[...] = h2

    tm = h2.shape[0]
    lane_f = lax.broadcasted_iota(jnp.int32, (1, LANES), 1).astype(F32)
    neg_inf = jnp.float32(-jnp.inf)
    logits = _dot_hi(h2, rwh_ref[...], rwl_ref[...]) + rb_ref[...]
    work = jnp.where(lane_f < float(n_experts), logits, neg_inf)
    vals, idxs, hots = [], [], []
    for _ in range(TOP_K):
        m = jnp.max(work, axis=1, keepdims=True)
        idx = jnp.min(jnp.where(work == m, lane_f, float(LANES)), axis=1, keepdims=True)
        hot = lane_f == idx
        vals.append(m)
        idxs.append(idx)
        hots.append(hot)
        work = jnp.where(hot, neg_inf, work)
    exps = [jnp.exp(v - vals[0]) for v in vals]
    denom = exps[0]
    for e in exps[1:]:
        denom = denom + e
    sel = hots[0].astype(F32)
    for hot in hots[1:]:
        sel = sel + hot.astype(F32)
    before = jnp.dot(ltri_ref[...], sel.astype(BF16), preferred_element_type=F32) + cnt_ref[...]
    ranks = [jnp.sum(jnp.where(hot, before, 0.0), axis=1, keepdims=True) for hot in hots]
    cnt_ref[...] = cnt_ref[...] + jnp.sum(sel, axis=0, keepdims=True)
    slot = lax.broadcasted_iota(jnp.int32, (tm, TOP_K), 1)

    def pack(cols):
        out = cols[TOP_K - 1]
        for k in range(TOP_K - 2, -1, -1):
            out = jnp.where(slot == k, cols[k], out)
        return out

    ei_ref[...] = pack(idxs).astype(jnp.int32)
    gt_ref[...] = pack([e / denom for e in exps])
    rk_ref[...] = pack(ranks).astype(jnp.int32)


def _post(o_ctx, o_lat, pg, x, mod, hg, bd, w_out, n2g, rw_hi, rw_lo, rb, ltri, n_experts, mod_map, tm):
    t, d = x.shape
    nct = o_ctx[0].shape[0] // tm
    row = lambda i: (i, 0)
    const = lambda i: (0, 0)
    ctx_row = lambda i: (jnp.minimum(i, nct - 1), 0)
    lat_row = lambda i: (jnp.maximum(i - nct, 0), 0)
    return pl.pallas_call(
        functools.partial(_post_kernel, n_experts, nct),
        grid=(t // tm,),
        in_specs=[pl.BlockSpec((tm, d), ctx_row), pl.BlockSpec((tm, d), ctx_row),
                  pl.BlockSpec((tm, d), lat_row), pl.BlockSpec((tm, d), lat_row),
                  pl.BlockSpec((tm, d), row),
                  pl.BlockSpec((tm, d), row), pl.BlockSpec((1, 8, d), mod_map),
                  pl.BlockSpec((1, d), const), pl.BlockSpec((d, d), const), pl.BlockSpec((d, d), const),
                  pl.BlockSpec((1, d), const), pl.BlockSpec((d, LANES), const),
                  pl.BlockSpec((d, LANES), const), pl.BlockSpec((1, LANES), const),
                  pl.BlockSpec((tm, tm), const)],
        out_specs=[pl.BlockSpec((tm, d), row), pl.BlockSpec((tm, d), row),
                   pl.BlockSpec((tm, TOP_K), row), pl.BlockSpec((tm, TOP_K), row),
                   pl.BlockSpec((tm, TOP_K), row), pl.BlockSpec((1, LANES), const)],
        out_shape=[jax.ShapeDtypeStruct((t, d), F32), jax.ShapeDtypeStruct((t, d), F32),
                   jax.ShapeDtypeStruct((t, TOP_K), jnp.int32), jax.ShapeDtypeStruct((t, TOP_K), F32),
                   jax.ShapeDtypeStruct((t, TOP_K), jnp.int32), jax.ShapeDtypeStruct((1, LANES), F32)],
        compiler_params=pltpu.CompilerParams(dimension_semantics=("arbitrary",),
                                             vmem_limit_bytes=VMEM_LIMIT),
        name="post_mix",
    )(o_ctx[0], o_ctx[1], o_lat[0], o_lat[1], pg, x, mod, hg, bd, w_out, n2g, rw_hi, rw_lo, rb, ltri)


def _dispatch_kernel(tm, e_ref, r_ref, ps_ref, h_ref, xb_in, xb_out, sem):
    del xb_in

    def row_copy(t, k):
        a = t * TOP_K + k
        dst = ps_ref[e_ref[a]] + r_ref[a]
        return pltpu.make_async_copy(h_ref.at[pl.ds(t, 1)], xb_out.at[pl.ds(dst, 1)], sem)

    def start(t, c):
        for k in range(TOP_K):
            row_copy(t, k).start()
        return c

    def wait(t, c):
        for k in range(TOP_K):
            row_copy(t, k).wait()
        return c

    lax.fori_loop(0, tm, start, 0, unroll=8)
    lax.fori_loop(0, tm, wait, 0, unroll=8)


def _dispatch(h2, e_flat, r_flat, pad_start, xb_zero, tm):
    t, d = h2.shape
    return pl.pallas_call(
        functools.partial(_dispatch_kernel, tm),
        grid=(t // tm,),
        in_specs=[pl.BlockSpec((tm * TOP_K,), lambda i: (i,), memory_space=pltpu.SMEM),
                  pl.BlockSpec((tm * TOP_K,), lambda i: (i,), memory_space=pltpu.SMEM),
                  pl.BlockSpec(memory_space=pltpu.SMEM),
                  pl.BlockSpec((tm, d), lambda i: (i, 0)),
                  pl.BlockSpec(memory_space=pl.ANY)],
        out_specs=pl.BlockSpec(memory_space=pl.ANY),
        out_shape=jax.ShapeDtypeStruct(xb_zero.shape, xb_zero.dtype),
        scratch_shapes=[pltpu.SemaphoreType.DMA(())],
        input_output_aliases={4: 0},
        compiler_params=pltpu.CompilerParams(dimension_semantics=("arbitrary",),
                                             vmem_limit_bytes=VMEM_LIMIT),
        name="moe_dispatch",
    )(e_flat, r_flat, pad_start, h2, xb_zero)


def _moe_kernel(be_ref, nu_ref, xb_ref, wgu_ref, bgu_ref, wdn_ref, bdn_ref, o_ref, wgu_b, wdn_b):
    i = pl.program_id(0)
    d_ff = wdn_ref.shape[1]
    active = i < nu_ref[0]
    new_expert = jnp.logical_or(i == 0, be_ref[i] != be_ref[jnp.maximum(i - 1, 0)])

    @pl.when(jnp.logical_and(active, new_expert))
    def _():
        wgu_b[...] = wgu_ref[0].astype(BF16)
        wdn_b[...] = wdn_ref[0].astype(BF16)

    @pl.when(active)
    def _():
        gu = jnp.dot(xb_ref[...].astype(BF16), wgu_b[...], preferred_element_type=F32) + bgu_ref[0]
        gate = jnp.minimum(gu[:, :d_ff], SWIGLU_LIMIT)
        up = jnp.clip(gu[:, d_ff:], -SWIGLU_LIMIT, SWIGLU_LIMIT)
        act = gate * _sigmoid(SWIGLU_ALPHA * gate) * (up + 1.0)
        o_ref[...] = jnp.dot(act.astype(BF16), wdn_b[...], preferred_element_type=F32) + bdn_ref[0]

    @pl.when(jnp.logical_not(active))
    def _():
        o_ref[...] = jnp.zeros_like(o_ref)


def _moe(xb, block_e, n_used, w_gu, b_gu, w_dn, b_dn, bm):
    rows, d = xb.shape
    n_e, _, n_gu = w_gu.shape
    d_ff = w_dn.shape[1]
    grid_spec = pltpu.PrefetchScalarGridSpec(
        num_scalar_prefetch=2,
        grid=(rows // bm,),
        in_specs=[pl.BlockSpec((bm, d), lambda i, be, nu: (i, 0)),
                  pl.BlockSpec((1, d, n_gu), lambda i, be, nu: (be[i], 0, 0)),
                  pl.BlockSpec((1, 1, n_gu), lambda i, be, nu: (be[i], 0, 0)),
                  pl.BlockSpec((1, d_ff, d), lambda i, be, nu: (be[i], 0, 0)),
                  pl.BlockSpec((1, 1, d), lambda i, be, nu: (be[i], 0, 0))],
        out_specs=pl.BlockSpec((bm, d), lambda i, be, nu: (i, 0)),
        scratch_shapes=[pltpu.VMEM((d, n_gu), BF16), pltpu.VMEM((d_ff, d), BF16)],
    )
    return pl.pallas_call(
        _moe_kernel,
        grid_spec=grid_spec,
        out_shape=jax.ShapeDtypeStruct((rows, d), F32),
        compiler_params=pltpu.CompilerParams(dimension_semantics=("arbitrary",),
                                             vmem_limit_bytes=VMEM_LIMIT),
        name="moe_ffn",
    )(block_e, n_used, xb, w_gu, b_gu.reshape(n_e, 1, n_gu), w_dn, b_dn.reshape(n_e, 1, d))


def _combine_kernel(final, tm, e_ref, r_ref, ps_ref, x1_ref, gt_ref, mod_ref, fg_ref, yb_ref,
                    o_ref, ybuf, sem):
    def row_copy(t, k):
        a = t * TOP_K + k
        src = ps_ref[e_ref[a]] + r_ref[a]
        return pltpu.make_async_copy(yb_ref.at[pl.ds(src, 1)], ybuf.at[k, pl.ds(t, 1)], sem)

    def start(t, c):
        for k in range(TOP_K):
            row_copy(t, k).start()
        return c

    def wait(t, c):
        for k in range(TOP_K):
            row_copy(t, k).wait()
        return c

    lax.fori_loop(0, tm, start, 0, unroll=8)
    lax.fori_loop(0, tm, wait, 0, unroll=8)

    gt = gt_ref[...]
    y = gt[:, 0:1] * ybuf[0]
    for k in range(1, TOP_K):
        y = y + gt[:, k:k + 1] * ybuf[k]
    x2 = x1_ref[...] + mod_ref[0, 5:6, :] * y
    if final:
        x2 = x2 * lax.rsqrt(jnp.mean(x2 * x2, axis=-1, keepdims=True) + EPS) * fg_ref[...]
    o_ref[...] = x2


def _combine(x1, yb, e_flat, r_flat, pad_start, gates, mod, final_g, final, mod_map, tm):
    t, d = x1.shape
    return pl.pallas_call(
        functools.partial(_combine_kernel, final, tm),
        grid=(t // tm,),
        in_specs=[pl.BlockSpec((tm * TOP_K,), lambda i: (i,), memory_space=pltpu.SMEM),
                  pl.BlockSpec((tm * TOP_K,), lambda i: (i,), memory_space=pltpu.SMEM),
                  pl.BlockSpec(memory_space=pltpu.SMEM),
                  pl.BlockSpec((tm, d), lambda i: (i, 0)),
                  pl.BlockSpec((tm, TOP_K), lambda i: (i, 0)),
                  pl.BlockSpec((1, 8, d), mod_map),
                  pl.BlockSpec((1, d), lambda i: (0, 0)),
                  pl.BlockSpec(memory_space=pl.ANY)],
        out_specs=pl.BlockSpec((tm, d), lambda i: (i, 0)),
        out_shape=jax.ShapeDtypeStruct((t, d), F32),
        scratch_shapes=[pltpu.VMEM((TOP_K, tm, d), F32), pltpu.SemaphoreType.DMA(())],
        compiler_params=pltpu.CompilerParams(dimension_semantics=("arbitrary",),
                                             vmem_limit_bytes=VMEM_LIMIT),
        name="moe_combine",
    )(e_flat, r_flat, pad_start, x1, gates, mod, final_g, yb)


def _rope_tables(n_tokens):
    rows = n_tokens // GRID_W
    r, c = jnp.meshgrid(jnp.arange(rows), jnp.arange(GRID_W), indexing='ij')
    nf = 16
    inv = ROPE_BASE ** (-jnp.arange(nf, dtype=F32) / nf)
    ang = jnp.concatenate([r.reshape(-1, 1).astype(F32) * inv, c.reshape(-1, 1).astype(F32) * inv], axis=-1)
    cos, sin = jnp.cos(ang), jnp.sin(ang)
    cos_t = jnp.tile(jnp.concatenate([cos, cos], axis=-1), (1, 4))
    sin_t = jnp.tile(jnp.concatenate([-sin, sin], axis=-1), (1, 4))
    return cos_t, sin_t


def _pad_lanes(a, width=LANES):
    return jnp.pad(a, [(0, 0)] * (a.ndim - 1) + [(0, width - a.shape[-1])])


def kernel(x_prompt, x_sample, c, state_mlstm_C, state_mlstm_n, state_mlstm_m, state_ret, state_gla, state_ssd,
           c_ctx, ada_w, ada_b, norm1_g, norm2_g, final_g, w_in, w_out,
           mlstm_i_b, mlstm_f_b, mlstm_norm_g, ret_decay, ret_norm_g,
           gla_gate_w, gla_gate_b, gla_norm_g,
           ssd_conv_w, ssd_conv_b, ssd_dt_b, ssd_A_log, ssd_D, ssd_norm_g,
           router_w, router_b, moe_w_gu, moe_b_gu, moe_w_down, moe_b_down):
    depth = w_in.shape[0]
    bp, lp, d = x_prompt.shape
    bs, ls, _ = x_sample.shape
    n_e = router_w.shape[-1]
    tm = 256
    bm = 256

    offs = np.cumsum([0, 256, 256, 256, 256, 8, 8, 256, 256, 256, 256, 128, 128, 256, 256, 32, 256, 512, 8])
    (o_mq, o_mk, o_mv, o_mo, o_mi, o_mf, o_rq, o_rk, o_rv, o_rg, o_gq, o_gk, o_gv, o_gr, o_glr,
     o_sz, o_sxbc, o_sdt, _) = [int(v) for v in offs]

    def cols(a, n):
        return w_in[:, :, a:a + n]

    small_w = jnp.concatenate([cols(o_mi, 8), cols(o_mf, 8), cols(o_sdt, 8),
                               jnp.zeros((depth, d, 8), F32), cols(o_glr, 32),
                               jnp.zeros((depth, d, LANES - 64), F32)], axis=-1)
    w_main = jnp.concatenate([cols(o_mq, 256), cols(o_mk, 256), cols(o_mv, 256),
                              cols(o_rq, 256), cols(o_rk, 256), cols(o_rv, 256),
                              cols(o_gq, 128), cols(o_gk, 128), cols(o_gv, 256),
                              cols(o_sxbc, 512), small_w], axis=-1).astype(BF16)
    w_gate = jnp.concatenate([cols(o_mo, 256), cols(o_rg, 256), cols(o_gr, 256), cols(o_sz, 256)],
                             axis=-1).astype(BF16)
    w_out_b = w_out.astype(BF16)
    rw = _pad_lanes(router_w)
    rw_hi = rw.astype(BF16)
    rw_lo = (rw - rw_hi.astype(F32)).astype(BF16)
    rb = _pad_lanes(router_b)[:, None, :]
    hg = jnp.concatenate([mlstm_norm_g, ret_norm_g, gla_norm_g, ssd_norm_g], axis=-1)[:, None, :]
    gidx = np.arange(d) // 64
    bd = jnp.asarray(gidx[:, None] == gidx[None, :], dtype=BF16)

    small_b = _pad_lanes(jnp.concatenate([mlstm_i_b.reshape(depth, 8), mlstm_f_b.reshape(depth, 8),
                                          ssd_dt_b.reshape(depth, 8)], axis=-1))[:, None, :]
    a_row = _pad_lanes(jnp.concatenate([jnp.zeros((depth, 16), F32), ssd_A_log.reshape(depth, 8)],
                                       axis=-1))[:, None, :]
    ret_row = _pad_lanes(ret_decay.reshape(depth, 8))[:, None, :]
    wg = jnp.zeros((depth, 2, LANES, LANES), F32)
    wg = wg.at[:, 0, S_GLR:S_GLR + 16, :].set(gla_gate_w[:, 0])
    wg = wg.at[:, 1, S_GLR + 16:S_GLR + 32, :].set(gla_gate_w[:, 1])
    wg = wg.astype(BF16)
    gb = gla_gate_b[:, :, None, :]
    d_row = jnp.repeat(ssd_D, 64, axis=-1)[:, None, :]
    conv_b = ssd_conv_b[:, None, :]

    n_cond = 1 + bs
    pad_rows = (-n_cond) % 8
    conds = jnp.concatenate([c_ctx[None, :], c, jnp.zeros((pad_rows, d), F32)], axis=0)
    mod_all = _ada_mod(conds, ada_w, ada_b).reshape(depth, n_cond + pad_rows, 6, d)
    mod_all = jnp.pad(mod_all, ((0, 0), (0, 0), (0, 2), (0, 0)))

    rope = _rope_tables(ls)

    def aug(cm, nm):
        return jnp.concatenate([cm, nm[..., None], jnp.zeros(cm.shape[:-1] + (63,), F32)], axis=-1)

    def m_rows(mv):
        return jnp.broadcast_to(mv.reshape(mv.shape[0], 8, 1), (mv.shape[0], 8, LANES))

    zero = lambda *s: jnp.zeros(s, F32)
    init_ctx = (zero(bp, 2, 4, 64, 128), zero(bp, 8, LANES), zero(bp, 2, 4, 64, 64),
                zero(bp, 2, 4, 64, 32), zero(bp, 2, 4, 64, 64))

    t_ctx = bp * lp
    t_tot = t_ctx + bs * ls
    x = jnp.concatenate([x_prompt.reshape(t_ctx, d), x_sample.reshape(bs * ls, d)], axis=0)
    mod_map = _mod_map(t_ctx // tm, ls // tm)
    ltri = jnp.asarray(np.tril(np.ones((tm, tm), np.float32), -1), dtype=BF16)
    n_assign = t_tot * TOP_K
    n_blocks = n_assign // bm + n_e
    rows = n_blocks * bm
    finals = []
    for l in range(depth):
        params = (small_b[l], a_row[l], ret_row[l], wg[l], gb[l], ssd_conv_w[l], conv_b[l], d_row[l])
        init_lat = (aug(state_mlstm_C[:, l], state_mlstm_n[:, l]), m_rows(state_mlstm_m[:, l]),
                    state_ret[:, l], jnp.swapaxes(state_gla[:, l], -1, -2), state_ssd[:, l])
        pm, pg = _proj(x, mod_all[l], norm1_g[l][None, :], w_main[l], w_gate[l], mod_map, tm)
        res_ctx = _mixers(pm, init_ctx, params, None, bp, lp, 0)
        res_lat = _mixers(pm, init_lat, params, rope, bs, ls, t_ctx)
        finals.append(res_ctx[2:])
        x1, h2, e_idx, gates, rank, counts = _post(
            res_ctx[:2], res_lat[:2], pg, x, mod_all[l], hg[l], bd, w_out_b[l], norm2_g[l][None, :],
            rw_hi[l], rw_lo[l], rb[l], ltri, n_e, mod_map, tm)

        cnt = counts[0, :n_e].astype(jnp.int32)
        padded = (cnt + bm - 1) // bm * bm
        pad_end = jnp.cumsum(padded)
        pad_start = (pad_end - padded).astype(jnp.int32)
        block_e = jnp.minimum(jnp.searchsorted(pad_end, jnp.arange(n_blocks, dtype=pad_end.dtype) * bm,
                                               side='right'), n_e - 1).astype(jnp.int32)
        n_used = (pad_end[-1] // bm).astype(jnp.int32).reshape(1)
        e_flat = e_idx.reshape(n_assign)
        r_flat = rank.reshape(n_assign)

        xb = _dispatch(h2, e_flat, r_flat, pad_start, jnp.zeros((rows, d), F32), tm)
        yb = _moe(xb, block_e, n_used, moe_w_gu[l], moe_b_gu[l], moe_w_down[l], moe_b_down[l], bm)
        x = _combine(x1, yb, e_flat, r_flat, pad_start, gates, mod_all[l], final_g[None, :],
                     l == depth - 1, mod_map, tm)

    y_prompt = x[:t_ctx].reshape(bp, lp, d)
    y_sample = x[t_ctx:].reshape(bs, ls, d)
    new_c = jnp.stack([f[0][..., :64] for f in finals], axis=1)
    new_n = jnp.stack([f[0][..., 64] for f in finals], axis=1)
    new_m = jnp.stack([f[1][:, :, 0].reshape(bp, 2, 4) for f in finals], axis=1)
    new_ret = jnp.stack([f[2] for f in finals], axis=1)
    new_gla = jnp.stack([jnp.swapaxes(f[3], -1, -2) for f in finals], axis=1)
    new_ssd = jnp.stack([f[4] for f in finals], axis=1)
    return (y_prompt, y_sample, new_c, new_n, new_m, new_ret, new_gla, new_ssd)
```

```python
import functools

import numpy as np
import jax
import jax.numpy as jnp
from jax import lax
from jax.experimental import pallas as pl
from jax.experimental.pallas import tpu as pltpu

F32 = jnp.float32
BF16 = jnp.bfloat16

EPS = 1e-6
CHUNK = 128
GRID_W = 64
ROPE_BASE = 10000.0
GLA_TAU = 16.0
TOP_K = 4
SWIGLU_LIMIT = 7.0
SWIGLU_ALPHA = 1.702
LANES = 128
SUBLANES = 8
VMEM_LIMIT = 56 * 1024 * 1024

MQ, MK, MV = 0, 256, 512
RQ, RK, RV = 768, 1024, 1280
GQ, GK, GV = 1536, 1664, 1792
SXBC = 2048
SMALL = 2560
P_MAIN = 2688
S_MI, S_MF, S_DT, S_DTRAW, S_GLR = 0, 8, 16, 24, 32


def _softplus(x):
    return jnp.maximum(x, 0.0) + jnp.log1p(jnp.exp(-jnp.abs(x)))


def _sigmoid(x):
    return 1.0 / (1.0 + jnp.exp(-x))


def _dot(a, b):
    return jnp.dot(a.astype(BF16), b.astype(BF16), preferred_element_type=F32)


def _dot_nt(a, b):
    return lax.dot_general(a.astype(BF16), b.astype(BF16), (((1,), (1,)), ((), ())),
                           preferred_element_type=F32)


def _split3(x):
    x1 = x.astype(BF16)
    r1 = x - x1.astype(F32)
    x2 = r1.astype(BF16)
    r2 = r1 - x2.astype(F32)
    return x1, x2, r2.astype(BF16)


def _mask_dot(mask_bf16, x):
    x1, x2, x3 = _split3(x)
    d = functools.partial(jnp.dot, preferred_element_type=F32)
    return (d(mask_bf16, x3) + d(mask_bf16, x2)) + d(mask_bf16, x1)


def _dot_mask(x, mask_bf16):
    x1, x2, x3 = _split3(x)
    d = functools.partial(jnp.dot, preferred_element_type=F32)
    return (d(x3, mask_bf16) + d(x2, mask_bf16)) + d(x1, mask_bf16)


def _dot_spread(x, sel_bf16):
    x1 = x.astype(BF16)
    x2 = (x - x1.astype(F32)).astype(BF16)
    d = functools.partial(jnp.dot, preferred_element_type=F32)
    return d(x2, sel_bf16) + d(x1, sel_bf16)


def _dot_hi(a, b_hi, b_lo):
    a1 = a.astype(BF16)
    a2 = (a - a1.astype(F32)).astype(BF16)
    d = functools.partial(jnp.dot, preferred_element_type=F32)
    return (d(a1, b_lo) + d(a2, b_hi)) + d(a1, b_hi)


def _ada_kernel(c_ref, w_ref, b_ref, o_ref):
    c = c_ref[...]
    s = c * _sigmoid(c)
    w = w_ref[0]
    w_hi = w.astype(BF16)
    w_lo = (w - w_hi.astype(F32)).astype(BF16)
    o_ref[0] = _dot_hi(s, w_hi, w_lo) + b_ref[0]


def _ada_mod(conds, ada_w, ada_b):
    depth, d, n6 = ada_w.shape
    rows = conds.shape[0]
    tn = 1536
    return pl.pallas_call(
        _ada_kernel,
        grid=(depth, n6 // tn),
        in_specs=[pl.BlockSpec((rows, d), lambda l, j: (0, 0)),
                  pl.BlockSpec((1, d, tn), lambda l, j: (l, 0, j)),
                  pl.BlockSpec((1, 1, tn), lambda l, j: (l, 0, j))],
        out_specs=pl.BlockSpec((1, rows, tn), lambda l, j: (l, 0, j)),
        out_shape=jax.ShapeDtypeStruct((depth, rows, n6), F32),
        compiler_params=pltpu.CompilerParams(dimension_semantics=("arbitrary", "arbitrary"),
                                             vmem_limit_bytes=VMEM_LIMIT),
        name="ada_mod",
    )(conds, ada_w, ada_b.reshape(depth, 1, n6))


def _proj_kernel(x_ref, mod_ref, g_ref, wm_ref, wsh_ref, wsl_ref, wg_ref, pm_ref, pg_ref):
    x = x_ref[...]
    y = x * lax.rsqrt(jnp.mean(x * x, axis=-1, keepdims=True) + EPS)
    h = (y * g_ref[...]) * (1.0 + mod_ref[0, 1:2, :]) + mod_ref[0, 0:1, :]
    hb = h.astype(BF16)
    pm_ref[:, 0:SMALL] = jnp.dot(hb, wm_ref[...], preferred_element_type=F32)
    pm_ref[:, SMALL:P_MAIN] = _dot_hi(h, wsh_ref[...], wsl_ref[...])
    pg_ref[...] = jnp.dot(hb, wg_ref[...], preferred_element_type=F32)


def _mod_map(n_ctx_tiles, tiles_per_latent_seq):
    def index(i, *_):
        return (jnp.where(i < n_ctx_tiles, 0, 1 + (i - n_ctx_tiles) // tiles_per_latent_seq), 0, 0)
    return index


def _proj(x, mod, g, w_main, ws_hi, ws_lo, w_gate, mod_map, tm):
    t, d = x.shape
    const = lambda i: (0, 0)
    return pl.pallas_call(
        _proj_kernel,
        grid=(t // tm,),
        in_specs=[pl.BlockSpec((tm, d), lambda i: (i, 0)),
                  pl.BlockSpec((1, 8, d), mod_map),
                  pl.BlockSpec((1, d), const),
                  pl.BlockSpec(w_main.shape, const),
                  pl.BlockSpec(ws_hi.shape, const),
                  pl.BlockSpec(ws_lo.shape, const),
                  pl.BlockSpec(w_gate.shape, const)],
        out_specs=[pl.BlockSpec((tm, P_MAIN), lambda i: (i, 0)),
                   pl.BlockSpec((tm, w_gate.shape[1]), lambda i: (i, 0))],
        out_shape=[jax.ShapeDtypeStruct((t, P_MAIN), F32),
                   jax.ShapeDtypeStruct((t, w_gate.shape[1]), F32)],
        compiler_params=pltpu.CompilerParams(dimension_semantics=("arbitrary",),
                                             vmem_limit_bytes=VMEM_LIMIT),
        name="in_proj",
    )(x, mod, g, w_main, ws_hi, ws_lo, w_gate)


def _mixer_direction(d, c_idx, n_chunks, use_rope, pm, hprev, hnext, cos_ref, sin_ref,
                     small_b, a_row, ret_rep, wg, gb, conv_w, conv_b, d_row, spread_b, spread_dt,
                     o_ref, mS, mm, rS, gS, sS, lat_s, cst_s):
    C = CHUNK
    last = C - 1 if d == 0 else 0
    row = lax.broadcasted_iota(jnp.int32, (C, C), 0)
    col = lax.broadcasted_iota(jnp.int32, (C, C), 1)
    mask = (col <= row) if d == 0 else (col >= row)
    mask_t = (row <= col) if d == 0 else (row >= col)
    tri = mask.astype(BF16)
    tri_t_ones = jnp.concatenate([mask_t.astype(BF16), jnp.ones((C, C), BF16)], axis=1)
    neg_inf = jnp.float32(-jnp.inf)
    lane = lax.broadcasted_iota(jnp.int32, (1, LANES), 1)
    even = lane < 64
    pair_sel = (even, jnp.logical_not(even))
    rows_of = (slice(0, 64), slice(64, 128))

    small = pm[:, SMALL:SMALL + LANES]
    pre = small + small_b[...]
    sp = _softplus(pre)
    lsg = -_softplus(-pre)
    a_neg = -jnp.exp(a_row[...])
    la = jnp.where(lane < S_MF, pre,
                   jnp.where(lane < S_DT, lsg,
                             jnp.where(lane < S_DTRAW, sp * a_neg, 0.0)))
    la_t = la.T
    lat_s[d] = la_t
    cst_s[d] = _dot_mask(la_t, tri_t_ones)
    cs = _mask_dot(tri, la)
    b_spread = _dot_spread(cs, spread_b[d])
    dt_spread = _dot_spread(sp, spread_dt[d])

    q_all = pm[:, MQ:MQ + 256] * (64.0 ** -0.5)
    k_all = pm[:, MK:MK + 256]
    v_all = pm[:, MV:MV + 256]
    k_t = k_all.T
    ones_blk = jnp.ones((C, LANES), BF16)
    for p in range(2):
        ps = slice(128 * p, 128 * p + 128)
        q_pair = q_all[:, ps].astype(BF16)
        k_pair = k_all[:, ps]
        vx = jnp.concatenate([v_all[:, ps].astype(BF16), ones_blk], axis=1)
        outs = []
        for r in range(2):
            h = 2 * p + r
            ci = d * 4 + h
            b_full = b_spread[:, 128 * h:128 * h + 128]
            b_row = cst_s[d, S_MF + ci:S_MF + ci + 1, 0:C]
            bl_row = cst_s[d, S_MF + ci:S_MF + ci + 1, C:2 * C]
            u_row = lat_s[d, S_MI + ci:S_MI + ci + 1, :] - b_row
            ms_row = mm[0, ci:ci + 1, :]
            d_log = jnp.where(mask, b_full + u_row, neg_inf)
            inter = b_full + ms_row
            m_t = jnp.maximum(inter, jnp.max(d_log, axis=1, keepdims=True))
            w = jnp.exp(d_log - m_t)
            k_m = jnp.where(pair_sel[r], k_pair, 0.0)
            sc = _dot_nt(q_pair, k_m) * w
            w_inter = jnp.exp(inter - m_t)
            s_old = mS[0, d, h]
            intra = _dot(sc, vx)
            cross = _dot(q_pair, s_old)
            num = intra[:, 0:LANES] + w_inter * cross[:, 0:LANES]
            den = intra[:, LANES:] + w_inter * cross[:, LANES:]
            outs.append(num / jnp.maximum(jnp.abs(den), jnp.exp(-m_t)))
            g_row = bl_row + u_row
            m_new = jnp.maximum(bl_row + ms_row, jnp.max(g_row, axis=1, keepdims=True))
            wk_row = jnp.exp(g_row - m_new)
            decay = jnp.exp(bl_row + ms_row - m_new)
            upd = _dot(k_t[64 * h:64 * h + 64, :] * wk_row, vx)
            live = rows_of[r]
            mS[0, d, h, live, 0:LANES] = decay * s_old[live, 0:LANES] + upd[:, 0:LANES]
            mS[0, d, h, live, LANES:] = decay * s_old[live, LANES:] + upd[:, LANES:]
            mm[0, ci:ci + 1, :] = m_new
        o_ref[:, ps] = jnp.where(even, outs[0], outs[1])

    rq = pm[:, RQ:RQ + 256]
    rk = pm[:, RK:RK + 256] * (64.0 ** -0.5)
    rv = pm[:, RV:RV + 256]
    if use_rope:
        cosv = cos_ref[...]
        sinv = sin_ref[...]
        first = (lax.broadcasted_iota(jnp.int32, (1, 256), 1) % 64) < 32

        def rope(x):
            rot = jnp.where(first, pltpu.roll(x, 256 - 32, 1), pltpu.roll(x, 32, 1))
            return x * cosv + rot * sinv

        rq = rope(rq)
        rk = rope(rk)
    rk_t = rk.T
    adiff = jnp.abs(row - col).astype(F32)
    ri_f = lax.broadcasted_iota(jnp.int32, (C, LANES), 0).astype(F32)
    cj_f = lax.broadcasted_iota(jnp.int32, (1, C), 1).astype(F32)
    ret_la = -jnp.exp(ret_rep[...])
    for p in range(2):
        ps = slice(128 * p, 128 * p + 128)
        q_pair = rq[:, ps].astype(BF16)
        k_pair = rk[:, ps]
        v_pair = rv[:, ps].astype(BF16)
        outs = []
        for r in range(2):
            h = 2 * p + r
            ci = d * 4 + h
            la_h = ret_la[ci:ci + 1, :]
            dec = jnp.where(mask, jnp.exp(la_h * adiff), 0.0)
            if d == 0:
                eb = jnp.exp(la_h * (ri_f + 1.0))
                kw_row = jnp.exp(la_h * (float(C - 1) - cj_f))
            else:
                eb = jnp.exp(la_h * (float(C) - ri_f))
                kw_row = jnp.exp(la_h * cj_f)
            k_m = jnp.where(pair_sel[r], k_pair, 0.0)
            sc = _dot_nt(q_pair, k_m) * dec
            s_old = rS[0, d, h]
            outs.append(_dot(sc, v_pair) + eb * _dot(q_pair, s_old))
            live = rows_of[r]
            rS[0, d, h, live, :] = (jnp.exp(la_h * float(C)) * s_old[live, :]
                                    + _dot(rk_t[64 * h:64 * h + 64, :] * kw_row, v_pair))
        o_ref[:, 256 + 128 * p:256 + 128 * p + 128] = jnp.where(even, outs[0], outs[1])

    gate = _dot(small, wg[d]) + gb[d]
    lga = (-_softplus(-gate)) * (1.0 / GLA_TAU)
    b = _mask_dot(tri, lga)
    bl_row = b[last:last + 1, :]
    b_mid = b[C // 2:C // 2 + 1, :]
    gq = pm[:, GQ:GQ + 128]
    gk = pm[:, GK:GK + 128] * (32.0 ** -0.5)
    gv = pm[:, GV:GV + 256]
    qc = (gq * jnp.exp(b - b_mid)).astype(BF16)
    kc = gk * jnp.exp(b_mid - b)
    qe = (gq * jnp.exp(b)).astype(BF16)
    kd = gk * jnp.exp(bl_row - b)
    ebl = jnp.exp(bl_row)
    gv_t = gv.T
    key_head = lane // 32
    for p in range(2):
        v_pair = gv[:, 128 * p:128 * p + 128].astype(BF16)
        outs = []
        for r in range(2):
            h = 2 * p + r
            mine = key_head == h
            sc = jnp.where(mask, _dot_nt(qc, jnp.where(mine, kc, 0.0)), 0.0)
            st_old = gS[0, d, h]
            outs.append(_dot(sc, v_pair) + _dot_nt(qe, st_old))
            live = rows_of[r]
            gS[0, d, h, live, :] = (st_old[live, :] * ebl
                                    + _dot(gv_t[64 * h:64 * h + 64, :], jnp.where(mine, kd, 0.0)))
        o_ref[:, 512 + 128 * p:512 + 128 * p + 128] = jnp.where(even, outs[0], outs[1])

    xr = pm[:, SXBC:SXBC + 512]
    prev = jnp.where(c_idx > 0, hprev[SUBLANES - 1:SUBLANES, :], 0.0)
    nxt = jnp.where(c_idx < n_chunks - 1, hnext[0:1, :], 0.0)
    rowi = lax.broadcasted_iota(jnp.int32, (C, 512), 0)
    x_dn = jnp.where(rowi == 0, prev, pltpu.roll(xr, 1, 0))
    x_up = jnp.where(rowi == C - 1, nxt, pltpu.roll(xr, C - 1, 0))
    y = conv_w[0:1, :] * x_dn + conv_w[1:2, :] * xr + conv_w[2:3, :] * x_up + conv_b[...]
    xbc = y * _sigmoid(y)
    sx = xbc[:, 0:256]
    s_b = xbc[:, 256:384]
    s_c = xbc[:, 384:512].astype(BF16)
    sb_t = s_b.T
    xdt = sx * dt_spread
    for p in range(2):
        ps = slice(128 * p, 128 * p + 128)
        v_pair = xdt[:, ps].astype(BF16)
        raw = _dot_nt(s_c, jnp.where(pair_sel[p], s_b, 0.0))
        outs = []
        for r in range(2):
            h = 2 * p + r
            ci = d * 4 + h
            b_full = b_spread[:, 512 + 128 * h:512 + 128 * h + 128]
            b_row = cst_s[d, S_DT + ci:S_DT + ci + 1, 0:C]
            bl_row = cst_s[d, S_DT + ci:S_DT + ci + 1, C:2 * C]
            dec = jnp.exp(jnp.where(mask, b_full - b_row, neg_inf))
            s_old = sS[0, d, h]
            outs.append(_dot(raw * dec, v_pair) + jnp.exp(b_full) * _dot(s_c, s_old))
            live = rows_of[p]
            sS[0, d, h, live, :] = (jnp.exp(bl_row) * s_old[live, :]
                                    + _dot(sb_t[64 * p:64 * p + 64, :] * jnp.exp(bl_row - b_row), v_pair))
        o = jnp.where(even, outs[0], outs[1])
        if d == 0:
            o = o + d_row[:, ps] * sx[:, ps]
        o_ref[:, 768 + 128 * p:768 + 128 * p + 128] = o


def _mixer_kernel(n_chunks, use_rope, *refs):
    it = iter(refs)
    pm_f, pm_b, hpf, hnf, hpb, hnb = (next(it) for _ in range(6))
    if use_rope:
        cos_f, sin_f, cos_b, sin_b = (next(it) for _ in range(4))
    else:
        cos_f = sin_f = cos_b = sin_b = None
    mS0, mm0, r0, g0, s0 = (next(it) for _ in range(5))
    params = tuple(next(it) for _ in range(10))
    o_f, o_b, mS, mm, rS, gS, sS = (next(it) for _ in range(7))
    lat_s, cst_s = (next(it) for _ in range(2))

    s = pl.program_id(1)

    @pl.when(s == 0)
    def _():
        mS[...] = mS0[...]
        mm[...] = mm0[...]
        rS[...] = r0[...]
        gS[...] = g0[...]
        sS[...] = s0[...]

    _mixer_direction(0, s, n_chunks, use_rope, pm_f, hpf, hnf, cos_f, sin_f, *params,
                     o_f, mS, mm, rS, gS, sS, lat_s, cst_s)
    _mixer_direction(1, n_chunks - 1 - s, n_chunks, use_rope, pm_b, hpb, hnb, cos_b, sin_b, *params,
                     o_b, mS, mm, rS, gS, sS, lat_s, cst_s)


def _mixers(pm, init, params, rope, n_seq, seq_len, row0):
    C = CHUNK
    n = seq_len // C
    t = pm.shape[0]
    chunk0 = row0 // C
    rb = C // SUBLANES
    n_rb = t // SUBLANES
    xbc_blk = SXBC // 512
    use_rope = rope is not None

    def f_idx(b, s):
        return chunk0 + b * n + s

    def b_idx(b, s):
        return chunk0 + b * n + (n - 1 - s)

    in_specs = [
        pl.BlockSpec((C, P_MAIN), lambda b, s: (f_idx(b, s), 0)),
        pl.BlockSpec((C, P_MAIN), lambda b, s: (b_idx(b, s), 0)),
        pl.BlockSpec((SUBLANES, 512), lambda b, s: (jnp.maximum(f_idx(b, s) * rb - 1, 0), xbc_blk)),
        pl.BlockSpec((SUBLANES, 512), lambda b, s: (jnp.minimum(f_idx(b, s) * rb + rb, n_rb - 1), xbc_blk)),
        pl.BlockSpec((SUBLANES, 512), lambda b, s: (jnp.maximum(b_idx(b, s) * rb - 1, 0), xbc_blk)),
        pl.BlockSpec((SUBLANES, 512), lambda b, s: (jnp.minimum(b_idx(b, s) * rb + rb, n_rb - 1), xbc_blk)),
    ]
    args = [pm, pm, pm, pm, pm, pm]
    if use_rope:
        cos, sin = rope
        in_specs += [pl.BlockSpec((C, 256), lambda b, s: (s, 0)),
                     pl.BlockSpec((C, 256), lambda b, s: (s, 0)),
                     pl.BlockSpec((C, 256), lambda b, s: (n - 1 - s, 0)),
                     pl.BlockSpec((C, 256), lambda b, s: (n - 1 - s, 0))]
        args += [cos, sin, cos, sin]

    def state_spec(shape):
        blk = (1,) + tuple(shape[1:])
        nd = len(shape)
        return pl.BlockSpec(blk, lambda b, s: (b,) + (0,) * (nd - 1))

    for a in init:
        in_specs.append(state_spec(a.shape))
        args.append(a)
    for p in params:
        nd = p.ndim
        in_specs.append(pl.BlockSpec(p.shape, lambda b, s, nd=nd: (0,) * nd))
        args.append(p)

    t_out = n_seq * seq_len
    out_specs = [pl.BlockSpec((C, 1024), lambda b, s: (b * n + s, 0)),
                 pl.BlockSpec((C, 1024), lambda b, s: (b * n + (n - 1 - s), 0))]
    out_shape = [jax.ShapeDtypeStruct((t_out, 1024), F32), jax.ShapeDtypeStruct((t_out, 1024), F32)]
    for a in init:
        out_specs.append(state_spec(a.shape))
        out_shape.append(jax.ShapeDtypeStruct(a.shape, F32))

    scratch = [pltpu.VMEM((2, LANES, C), F32), pltpu.VMEM((2, LANES, 2 * C), F32)]
    return pl.pallas_call(
        functools.partial(_mixer_kernel, n, use_rope),
        grid=(n_seq, n),
        in_specs=in_specs,
        out_specs=out_specs,
        out_shape=out_shape,
        scratch_shapes=scratch,
        compiler_params=pltpu.CompilerParams(dimension_semantics=("arbitrary", "arbitrary"),
                                             vmem_limit_bytes=VMEM_LIMIT),
        name="mixers_rope" if use_rope else "mixers",
    )(*args)


def _post_kernel(n_experts, n_ctx_tiles, ofc_ref, obc_ref, ofl_ref, obl_ref, pg_ref, x_ref, mod_ref,
                 hg_ref, bd_ref, wo_ref, n2_ref, rwh_ref, rwl_ref, rb_ref, ltri_ref,
                 x1_ref, h2_ref, ei_ref, gt_ref, rk_ref, cnt_ref):
    @pl.when(pl.program_id(0) == 0)
    def _():
        cnt_ref[...] = jnp.zeros_like(cnt_ref)

    y = jnp.where(pl.program_id(0) < n_ctx_tiles, ofc_ref[...] + obc_ref[...], ofl_ref[...] + obl_ref[...])
    gt = pg_ref[...]
    lane = lax.broadcasted_iota(jnp.int32, (1, y.shape[1]), 1)
    sig = _sigmoid(gt)
    silu = gt * sig
    pre = y * jnp.where(lane >= 768, silu, 1.0)
    ms = jnp.dot((pre * pre).astype(BF16), bd_ref[...], preferred_element_type=F32) * (1.0 / 64.0)
    nrm = pre * lax.rsqrt(ms + EPS) * hg_ref[...]
    post = jnp.where(lane < 256, sig, jnp.where(lane < 768, silu, 1.0))
    mixed = jnp.dot((nrm * post).astype(BF16), wo_ref[...], preferred_element_type=F32)
    x1 = x_ref[...] + mod_ref[0, 2:3, :] * mixed
    x1_ref[...] = x1
    yn = x1 * lax.rsqrt(jnp.mean(x1 * x1, axis=-1, keepdims=True) + EPS)
    h2 = (yn * n2_ref[...]) * (1.0 + mod_ref[0, 4:5, :]) + mod_ref[0, 3:4, :]
    for s in range(h2_ref.shape[0]):
        h2_ref[s] = h2[:, LANES * s:LANES * s + LANES]

    tm = h2.shape[0]
    lane_f = lax.broadcasted_iota(jnp.int32, (1, LANES), 1).astype(F32)
    neg_inf = jnp.float32(-jnp.inf)
    logits = _dot_hi(h2, rwh_ref[...], rwl_ref[...]) + rb_ref[...]
    work = jnp.where(lane_f < float(n_experts), logits, neg_inf)
    vals, idxs, hots = [], [], []
    for _ in range(TOP_K):
        m = jnp.max(work, axis=1, keepdims=True)
        idx = jnp.min(jnp.where(work == m, lane_f, float(LANES)), axis=1, keepdims=True)
        hot = lane_f == idx
        vals.append(m)
        idxs.append(idx)
        hots.append(hot)
        work = jnp.where(hot, neg_inf, work)
    exps = [jnp.exp(v - vals[0]) for v in vals]
    denom = exps[0]
    for e in exps[1:]:
        denom = denom + e
    sel = hots[0].astype(F32)
    for hot in hots[1:]:
        sel = sel + hot.astype(F32)
    before = jnp.dot(ltri_ref[...], sel.astype(BF16), preferred_element_type=F32) + cnt_ref[...]
    ranks = [jnp.sum(jnp.where(hot, before, 0.0), axis=1, keepdims=True) for hot in hots]
    cnt_ref[...] = cnt_ref[...] + jnp.sum(sel, axis=0, keepdims=True)
    slot = lax.broadcasted_iota(jnp.int32, (tm, TOP_K), 1)

    def pack(cols):
        out = cols[TOP_K - 1]
        for k in range(TOP_K - 2, -1, -1):
            out = jnp.where(slot == k, cols[k], out)
        return out

    ei_ref[...] = pack(idxs).astype(jnp.int32)
    gt_ref[...] = pack([e / denom for e in exps])
    rk_ref[...] = pack(ranks).astype(jnp.int32)


def _post(o_ctx, o_lat, pg, x, mod, hg, bd, w_out, n2g, rw_hi, rw_lo, rb, ltri, n_experts, mod_map, tm):
    t, d = x.shape
    n_slab = d // LANES
    nct = o_ctx[0].shape[0] // tm
    row = lambda i: (i, 0)
    const = lambda i: (0, 0)
    ctx_row = lambda i: (jnp.minimum(i, nct - 1), 0)
    lat_row = lambda i: (jnp.maximum(i - nct, 0), 0)
    return pl.pallas_call(
        functools.partial(_post_kernel, n_experts, nct),
        grid=(t // tm,),
        in_specs=[pl.BlockSpec((tm, d), ctx_row), pl.BlockSpec((tm, d), ctx_row),
                  pl.BlockSpec((tm, d), lat_row), pl.BlockSpec((tm, d), lat_row),
                  pl.BlockSpec((tm, d), row),
                  pl.BlockSpec((tm, d), row), pl.BlockSpec((1, 8, d), mod_map),
                  pl.BlockSpec((1, d), const), pl.BlockSpec((d, d), const), pl.BlockSpec((d, d), const),
                  pl.BlockSpec((1, d), const), pl.BlockSpec((d, LANES), const),
                  pl.BlockSpec((d, LANES), const), pl.BlockSpec((1, LANES), const),
                  pl.BlockSpec((tm, tm), const)],
        out_specs=[pl.BlockSpec((tm, d), row), pl.BlockSpec((n_slab, tm, LANES), lambda i: (0, i, 0)),
                   pl.BlockSpec((tm, TOP_K), row), pl.BlockSpec((tm, TOP_K), row),
                   pl.BlockSpec((tm, TOP_K), row), pl.BlockSpec((1, LANES), const)],
        out_shape=[jax.ShapeDtypeStruct((t, d), F32), jax.ShapeDtypeStruct((n_slab, t, LANES), F32),
                   jax.ShapeDtypeStruct((t, TOP_K), jnp.int32), jax.ShapeDtypeStruct((t, TOP_K), F32),
                   jax.ShapeDtypeStruct((t, TOP_K), jnp.int32), jax.ShapeDtypeStruct((1, LANES), F32)],
        compiler_params=pltpu.CompilerParams(dimension_semantics=("arbitrary",),
                                             vmem_limit_bytes=VMEM_LIMIT),
        name="post_mix",
    )(o_ctx[0], o_ctx[1], o_lat[0], o_lat[1], pg, x, mod, hg, bd, w_out, n2g, rw_hi, rw_lo, rb, ltri)


def _dispatch_kernel(tm, dest_ref, h_ref, xb_in, xb_out, sem):
    del xb_in

    def row_copy(t, k):
        return pltpu.make_async_copy(h_ref.at[:, pl.ds(t, 1), :],
                                     xb_out.at[:, pl.ds(dest_ref[t * TOP_K + k], 1), :], sem)

    def start(t, c):
        for k in range(TOP_K):
            row_copy(t, k).start(priority=k % 2)
        return c

    def wait(t, c):
        for k in range(TOP_K):
            row_copy(t, k).wait()
        return c

    lax.fori_loop(0, tm, start, 0, unroll=8)
    lax.fori_loop(0, tm, wait, 0, unroll=8)


def _dispatch(h2, dest, xb_zero, tm):
    n_slab, t, _ = h2.shape
    return pl.pallas_call(
        functools.partial(_dispatch_kernel, tm),
        grid=(t // tm,),
        in_specs=[pl.BlockSpec((tm * TOP_K,), lambda i: (i,), memory_space=pltpu.SMEM),
                  pl.BlockSpec((n_slab, tm, LANES), lambda i: (0, i, 0)),
                  pl.BlockSpec(memory_space=pl.ANY)],
        out_specs=pl.BlockSpec(memory_space=pl.ANY),
        out_shape=jax.ShapeDtypeStruct(xb_zero.shape, xb_zero.dtype),
        scratch_shapes=[pltpu.SemaphoreType.DMA(())],
        input_output_aliases={2: 0},
        compiler_params=pltpu.CompilerParams(dimension_semantics=("arbitrary",),
                                             vmem_limit_bytes=VMEM_LIMIT),
        name="moe_dispatch",
    )(dest, h2, xb_zero)


def _moe_kernel(be_ref, nu_ref, xb_ref, wgu_ref, bgu_ref, wdn_ref, bdn_ref, o_ref, wgu_b, wdn_b):
    i = pl.program_id(0)
    d_ff = wdn_ref.shape[1]
    n_slab = xb_ref.shape[0]
    active = i < nu_ref[0]
    new_expert = jnp.logical_or(i == 0, be_ref[i] != be_ref[jnp.maximum(i - 1, 0)])

    @pl.when(jnp.logical_and(active, new_expert))
    def _():
        wgu_b[...] = wgu_ref[0].astype(BF16)
        wdn_b[...] = wdn_ref[0].astype(BF16)

    @pl.when(active)
    def _():
        x = jnp.concatenate([xb_ref[s].astype(BF16) for s in range(n_slab)], axis=1)
        gu = jnp.dot(x, wgu_b[...], preferred_element_type=F32) + bgu_ref[0]
        gate = jnp.minimum(gu[:, :d_ff], SWIGLU_LIMIT)
        up = jnp.clip(gu[:, d_ff:], -SWIGLU_LIMIT, SWIGLU_LIMIT)
        act = gate * _sigmoid(SWIGLU_ALPHA * gate) * (up + 1.0)
        out = jnp.dot(act.astype(BF16), wdn_b[...], preferred_element_type=F32) + bdn_ref[0]
        for s in range(n_slab):
            o_ref[s] = out[:, LANES * s:LANES * s + LANES]

    @pl.when(jnp.logical_not(active))
    def _():
        o_ref[...] = jnp.zeros_like(o_ref)


def _moe(xb, block_e, n_used, w_gu, b_gu, w_dn, b_dn, bm):
    n_slab, rows, _ = xb.shape
    n_e, d, n_gu = w_gu.shape
    d_ff = w_dn.shape[1]
    grid_spec = pltpu.PrefetchScalarGridSpec(
        num_scalar_prefetch=2,
        grid=(rows // bm,),
        in_specs=[pl.BlockSpec((n_slab, bm, LANES), lambda i, be, nu: (0, i, 0)),
                  pl.BlockSpec((1, d, n_gu), lambda i, be, nu: (be[i], 0, 0)),
                  pl.BlockSpec((1, 1, n_gu), lambda i, be, nu: (be[i], 0, 0)),
                  pl.BlockSpec((1, d_ff, d), lambda i, be, nu: (be[i], 0, 0)),
                  pl.BlockSpec((1, 1, d), lambda i, be, nu: (be[i], 0, 0))],
        out_specs=pl.BlockSpec((n_slab, bm, LANES), lambda i, be, nu: (0, i, 0)),
        scratch_shapes=[pltpu.VMEM((d, n_gu), BF16), pltpu.VMEM((d_ff, d), BF16)],
    )
    return pl.pallas_call(
        _moe_kernel,
        grid_spec=grid_spec,
        out_shape=jax.ShapeDtypeStruct((n_slab, rows, LANES), F32),
        compiler_params=pltpu.CompilerParams(dimension_semantics=("arbitrary",),
                                             vmem_limit_bytes=VMEM_LIMIT),
        name="moe_ffn",
    )(block_e, n_used, xb, w_gu, b_gu.reshape(n_e, 1, n_gu), w_dn, b_dn.reshape(n_e, 1, d))


def _combine_kernel(final, tm, dest_ref, x1_ref, gt_ref, mod_ref, fg_ref, yb_ref, o_ref, ybuf, sem):
    def row_copy(t, k):
        return pltpu.make_async_copy(yb_ref.at[:, pl.ds(dest_ref[t * TOP_K + k], 1), :],
                                     ybuf.at[k, :, pl.ds(t, 1), :], sem)

    def start(t, c):
        for k in range(TOP_K):
            row_copy(t, k).start(priority=k % 2)
        return c

    def wait(t, c):
        for k in range(TOP_K):
            row_copy(t, k).wait()
        return c

    lax.fori_loop(0, tm, start, 0, unroll=8)
    lax.fori_loop(0, tm, wait, 0, unroll=8)

    n_slab = ybuf.shape[1]
    gt = gt_ref[...]
    g_full = [jnp.broadcast_to(gt[:, k:k + 1], (tm, LANES)) for k in range(TOP_K)]
    slabs = []
    for s in range(n_slab):
        cs = slice(LANES * s, LANES * s + LANES)
        y = g_full[0] * ybuf[0, s]
        for k in range(1, TOP_K):
            y = y + g_full[k] * ybuf[k, s]
        slabs.append(x1_ref[:, cs] + mod_ref[0, 5:6, cs] * y)
    x2 = jnp.concatenate(slabs, axis=1)
    if final:
        x2 = x2 * lax.rsqrt(jnp.mean(x2 * x2, axis=-1, keepdims=True) + EPS) * fg_ref[...]
    o_ref[...] = x2


def _combine(x1, yb, dest, gates, mod, final_g, final, mod_map, tm):
    t, d = x1.shape
    n_slab = d // LANES
    return pl.pallas_call(
        functools.partial(_combine_kernel, final, tm),
        grid=(t // tm,),
        in_specs=[pl.BlockSpec((tm * TOP_K,), lambda i: (i,), memory_space=pltpu.SMEM),
                  pl.BlockSpec((tm, d), lambda i: (i, 0)),
                  pl.BlockSpec((tm, TOP_K), lambda i: (i, 0)),
                  pl.BlockSpec((1, 8, d), mod_map),
                  pl.BlockSpec((1, d), lambda i: (0, 0)),
                  pl.BlockSpec(memory_space=pl.ANY)],
        out_specs=pl.BlockSpec((tm, d), lambda i: (i, 0)),
        out_shape=jax.ShapeDtypeStruct((t, d), F32),
        scratch_shapes=[pltpu.VMEM((TOP_K, n_slab, tm, LANES), F32), pltpu.SemaphoreType.DMA(())],
        compiler_params=pltpu.CompilerParams(dimension_semantics=("arbitrary",),
                                             vmem_limit_bytes=VMEM_LIMIT),
        name="moe_combine",
    )(dest, x1, gates, mod, final_g, yb)


def _rope_tables(n_tokens):
    rows = n_tokens // GRID_W
    r, c = jnp.meshgrid(jnp.arange(rows), jnp.arange(GRID_W), indexing='ij')
    nf = 16
    inv = ROPE_BASE ** (-jnp.arange(nf, dtype=F32) / nf)
    ang = jnp.concatenate([r.reshape(-1, 1).astype(F32) * inv, c.reshape(-1, 1).astype(F32) * inv], axis=-1)
    cos, sin = jnp.cos(ang), jnp.sin(ang)
    cos_t = jnp.tile(jnp.concatenate([cos, cos], axis=-1), (1, 4))
    sin_t = jnp.tile(jnp.concatenate([-sin, sin], axis=-1), (1, 4))
    return cos_t, sin_t


def _pad_lanes(a, width=LANES):
    return jnp.pad(a, [(0, 0)] * (a.ndim - 1) + [(0, width - a.shape[-1])])


def _spread_matrices():
    sb = np.zeros((2, LANES, 1024), np.float32)
    sdt = np.zeros((2, LANES, 256), np.float32)
    for d in range(2):
        for h in range(4):
            ci = d * 4 + h
            sb[d, S_MF + ci, 128 * h:128 * h + 128] = 1.0
            sb[d, S_DT + ci, 512 + 128 * h:512 + 128 * h + 128] = 1.0
            sdt[d, S_DT + ci, 64 * h:64 * h + 64] = 1.0
    return jnp.asarray(sb, dtype=BF16), jnp.asarray(sdt, dtype=BF16)


def _pad_states(c_m, n_m, m_m, s_ret, s_gla, s_ssd):
    bsz = c_m.shape[0]
    mS = jnp.zeros((bsz, 2, 4, 128, 256), F32)
    rS = jnp.zeros((bsz, 2, 4, 128, 128), F32)
    gS = jnp.zeros((bsz, 2, 4, 128, 128), F32)
    sS = jnp.zeros((bsz, 2, 4, 128, 128), F32)
    for h in range(4):
        r, g = h % 2, h // 2
        lv = slice(64 * r, 64 * r + 64)
        mS = mS.at[:, :, h, lv, lv].set(c_m[:, :, h])
        mS = mS.at[:, :, h, lv, 128:].set(jnp.broadcast_to(n_m[:, :, h, :, None], (bsz, 2, 64, 128)))
        rS = rS.at[:, :, h, lv, lv].set(s_ret[:, :, h])
        gS = gS.at[:, :, h, lv, 32 * h:32 * h + 32].set(jnp.swapaxes(s_gla[:, :, h], -1, -2))
        sS = sS.at[:, :, h, 64 * g:64 * g + 64, lv].set(s_ssd[:, :, h])
    mm = jnp.broadcast_to(m_m.reshape(bsz, 8, 1), (bsz, 8, LANES))
    return mS, mm, rS, gS, sS


def _unpad_states(mS, mm, rS, gS, sS):
    bsz = mS.shape[0]
    c_m, n_m, s_ret, s_gla, s_ssd = [], [], [], [], []
    for h in range(4):
        r, g = h % 2, h // 2
        lv = slice(64 * r, 64 * r + 64)
        c_m.append(mS[:, :, h, lv, lv])
        n_m.append(mS[:, :, h, lv, 128])
        s_ret.append(rS[:, :, h, lv, lv])
        s_gla.append(jnp.swapaxes(gS[:, :, h, lv, 32 * h:32 * h + 32], -1, -2))
        s_ssd.append(sS[:, :, h, 64 * g:64 * g + 64, lv])
    stack = lambda xs: jnp.stack(xs, axis=2)
    return (stack(c_m), stack(n_m), mm[:, :, 0].reshape(bsz, 2, 4), stack(s_ret), stack(s_gla), stack(s_ssd))


def kernel(x_prompt, x_sample, c, state_mlstm_C, state_mlstm_n, state_mlstm_m, state_ret, state_gla, state_ssd,
           c_ctx, ada_w, ada_b, norm1_g, norm2_g, final_g, w_in, w_out,
           mlstm_i_b, mlstm_f_b, mlstm_norm_g, ret_decay, ret_norm_g,
           gla_gate_w, gla_gate_b, gla_norm_g,
           ssd_conv_w, ssd_conv_b, ssd_dt_b, ssd_A_log, ssd_D, ssd_norm_g,
           router_w, router_b, moe_w_gu, moe_b_gu, moe_w_down, moe_b_down):
    depth = w_in.shape[0]
    bp, lp, d = x_prompt.shape
    bs, ls, _ = x_sample.shape
    n_e = router_w.shape[-1]
    n_slab = d // LANES
    tm = 256
    bm = 256

    offs = np.cumsum([0, 256, 256, 256, 256, 8, 8, 256, 256, 256, 256, 128, 128, 256, 256, 32, 256, 512, 8])
    (o_mq, o_mk, o_mv, o_mo, o_mi, o_mf, o_rq, o_rk, o_rv, o_rg, o_gq, o_gk, o_gv, o_gr, o_glr,
     o_sz, o_sxbc, o_sdt, _) = [int(v) for v in offs]

    def cols(a, n):
        return w_in[:, :, a:a + n]

    small_w = jnp.concatenate([cols(o_mi, 8), cols(o_mf, 8), cols(o_sdt, 8),
                               jnp.zeros((depth, d, 8), F32), cols(o_glr, 32),
                               jnp.zeros((depth, d, LANES - 64), F32)], axis=-1)
    ws_hi = small_w.astype(BF16)
    ws_lo = (small_w - ws_hi.astype(F32)).astype(BF16)
    w_main = jnp.concatenate([cols(o_mq, 256), cols(o_mk, 256), cols(o_mv, 256),
                              cols(o_rq, 256), cols(o_rk, 256), cols(o_rv, 256),
                              cols(o_gq, 128), cols(o_gk, 128), cols(o_gv, 256),
                              cols(o_sxbc, 512)], axis=-1).astype(BF16)
    w_gate = jnp.concatenate([cols(o_mo, 256), cols(o_rg, 256), cols(o_gr, 256), cols(o_sz, 256)],
                             axis=-1).astype(BF16)
    w_out_b = w_out.astype(BF16)
    rw = _pad_lanes(router_w)
    rw_hi = rw.astype(BF16)
    rw_lo = (rw - rw_hi.astype(F32)).astype(BF16)
    rb = _pad_lanes(router_b)[:, None, :]
    hg = jnp.concatenate([mlstm_norm_g, ret_norm_g, gla_norm_g, ssd_norm_g], axis=-1)[:, None, :]
    gidx = np.arange(d) // 64
    bd = jnp.asarray(gidx[:, None] == gidx[None, :], dtype=BF16)

    small_b = _pad_lanes(jnp.concatenate([mlstm_i_b.reshape(depth, 8), mlstm_f_b.reshape(depth, 8),
                                          ssd_dt_b.reshape(depth, 8)], axis=-1))[:, None, :]
    a_row = _pad_lanes(jnp.concatenate([jnp.zeros((depth, 16), F32), ssd_A_log.reshape(depth, 8)],
                                       axis=-1))[:, None, :]
    ret_rep = jnp.broadcast_to(ret_decay.reshape(depth, 8, 1), (depth, 8, LANES))
    wg = jnp.zeros((depth, 2, LANES, LANES), F32)
    wg = wg.at[:, 0, S_GLR:S_GLR + 16, :].set(gla_gate_w[:, 0])
    wg = wg.at[:, 1, S_GLR + 16:S_GLR + 32, :].set(gla_gate_w[:, 1])
    wg = wg.astype(BF16)
    gb = gla_gate_b[:, :, None, :]
    d_row = jnp.repeat(ssd_D, 64, axis=-1)[:, None, :]
    conv_b = ssd_conv_b[:, None, :]
    spread_b, spread_dt = _spread_matrices()

    n_cond = 1 + bs
    pad_rows = (-n_cond) % 8
    conds = jnp.concatenate([c_ctx[None, :], c, jnp.zeros((pad_rows, d), F32)], axis=0)
    mod_all = _ada_mod(conds, ada_w, ada_b).reshape(depth, n_cond + pad_rows, 6, d)
    mod_all = jnp.pad(mod_all, ((0, 0), (0, 0), (0, 2), (0, 0)))

    rope = _rope_tables(ls)
    zero = lambda *s: jnp.zeros(s, F32)
    init_ctx = (zero(bp, 2, 4, 128, 256), zero(bp, 8, LANES), zero(bp, 2, 4, 128, 128),
                zero(bp, 2, 4, 128, 128), zero(bp, 2, 4, 128, 128))

    t_ctx = bp * lp
    t_tot = t_ctx + bs * ls
    x = jnp.concatenate([x_prompt.reshape(t_ctx, d), x_sample.reshape(bs * ls, d)], axis=0)
    mod_map = _mod_map(t_ctx // tm, ls // tm)
    ltri = jnp.asarray(np.tril(np.ones((tm, tm), np.float32), -1), dtype=BF16)
    n_assign = t_tot * TOP_K
    n_blocks = n_assign // bm + n_e
    rows = n_blocks * bm
    finals = []
    for l in range(depth):
        params = (small_b[l], a_row[l], ret_rep[l], wg[l], gb[l], ssd_conv_w[l], conv_b[l], d_row[l],
                  spread_b, spread_dt)
        init_lat = _pad_states(state_mlstm_C[:, l], state_mlstm_n[:, l], state_mlstm_m[:, l],
                               state_ret[:, l], state_gla[:, l], state_ssd[:, l])
        pm, pg = _proj(x, mod_all[l], norm1_g[l][None, :], w_main[l], ws_hi[l], ws_lo[l], w_gate[l],
                       mod_map, tm)
        res_ctx = _mixers(pm, init_ctx, params, None, bp, lp, 0)
        res_lat = _mixers(pm, init_lat, params, rope, bs, ls, t_ctx)
        finals.append(_unpad_states(*res_ctx[2:]))
        x1, h2, e_idx, gates, rank, counts = _post(
            res_ctx[:2], res_lat[:2], pg, x, mod_all[l], hg[l], bd, w_out_b[l], norm2_g[l][None, :],
            rw_hi[l], rw_lo[l], rb[l], ltri, n_e, mod_map, tm)

        cnt = counts[0, :n_e].astype(jnp.int32)
        padded = (cnt + bm - 1) // bm * bm
        pad_end = jnp.cumsum(padded)
        pad_start = (pad_end - padded).astype(jnp.int32)
        block_start = jnp.arange(n_blocks, dtype=jnp.int32) * bm
        block_e = jnp.minimum(jnp.sum((pad_end[None, :] <= block_start[:, None]).astype(jnp.int32), axis=1),
                              n_e - 1).astype(jnp.int32)
        n_used = (pad_end[-1] // bm).astype(jnp.int32).reshape(1)
        experts = jnp.arange(n_e, dtype=jnp.int32)
        seg_start = jnp.sum(jnp.where(e_idx[..., None] == experts, pad_start, 0), axis=-1)
        dest = (seg_start + rank).astype(jnp.int32).reshape(n_assign)

        xb = _dispatch(h2, dest, jnp.zeros((n_slab, rows, LANES), F32), tm)
        yb = _moe(xb, block_e, n_used, moe_w_gu[l], moe_b_gu[l], moe_w_down[l], moe_b_down[l], bm)
        x = _combine(x1, yb, dest, gates, mod_all[l], final_g[None, :], l == depth - 1, mod_map, tm)

    y_prompt = x[:t_ctx].reshape(bp, lp, d)
    y_sample = x[t_ctx:].reshape(bs, ls, d)
    outs = [jnp.stack([f[i] for f in finals], axis=1) for i in range(6)]
    return (y_prompt, y_sample, *outs)
```

```python
import functools

import numpy as np
import jax
import jax.numpy as jnp
from jax import lax
from jax.experimental import pallas as pl
from jax.experimental.pallas import tpu as pltpu

F32 = jnp.float32
BF16 = jnp.bfloat16

EPS = 1e-6
CHUNK = 128
GRID_W = 64
ROPE_BASE = 10000.0
GLA_TAU = 16.0
TOP_K = 4
SWIGLU_LIMIT = 7.0
SWIGLU_ALPHA = 1.702
LANES = 128
SUBLANES = 8
VMEM_LIMIT = 56 * 1024 * 1024

MQ, MK, MV = 0, 256, 512
RQ, RK, RV = 768, 1024, 1280
GQ, GK, GV = 1536, 1664, 1792
SXBC = 2048
SMALL = 2560
P_MAIN = 2688
S_MI, S_MF, S_DT, S_DTRAW, S_GLR = 0, 8, 16, 24, 32


def _softplus(x):
    return jnp.maximum(x, 0.0) + jnp.log1p(jnp.exp(-jnp.abs(x)))


def _sigmoid(x):
    return 1.0 / (1.0 + jnp.exp(-x))


def _dot(a, b):
    return jnp.dot(a.astype(BF16), b.astype(BF16), preferred_element_type=F32)


def _dot_nt(a, b):
    return lax.dot_general(a.astype(BF16), b.astype(BF16), (((1,), (1,)), ((), ())),
                           preferred_element_type=F32)


def _split3(x):
    x1 = x.astype(BF16)
    r1 = x - x1.astype(F32)
    x2 = r1.astype(BF16)
    r2 = r1 - x2.astype(F32)
    return x1, x2, r2.astype(BF16)


def _mask_dot(mask_bf16, x):
    x1, x2, x3 = _split3(x)
    d = functools.partial(jnp.dot, preferred_element_type=F32)
    return (d(mask_bf16, x3) + d(mask_bf16, x2)) + d(mask_bf16, x1)


def _dot_mask(x, mask_bf16):
    x1, x2, x3 = _split3(x)
    d = functools.partial(jnp.dot, preferred_element_type=F32)
    return (d(x3, mask_bf16) + d(x2, mask_bf16)) + d(x1, mask_bf16)


def _dot_spread(x, sel_bf16):
    x1 = x.astype(BF16)
    x2 = (x - x1.astype(F32)).astype(BF16)
    d = functools.partial(jnp.dot, preferred_element_type=F32)
    return d(x2, sel_bf16) + d(x1, sel_bf16)


def _ada_kernel(c_ref, w_ref, b_ref, o_ref):
    c = c_ref[...]
    o_ref[0] = _dot(c * _sigmoid(c), w_ref[0]) + b_ref[0]


def _ada_mod(conds, ada_w, ada_b):
    depth, d, n6 = ada_w.shape
    rows = conds.shape[0]
    tn = 1536
    return pl.pallas_call(
        _ada_kernel,
        grid=(depth, n6 // tn),
        in_specs=[pl.BlockSpec((rows, d), lambda l, j: (0, 0)),
                  pl.BlockSpec((1, d, tn), lambda l, j: (l, 0, j)),
                  pl.BlockSpec((1, 1, tn), lambda l, j: (l, 0, j))],
        out_specs=pl.BlockSpec((1, rows, tn), lambda l, j: (l, 0, j)),
        out_shape=jax.ShapeDtypeStruct((depth, rows, n6), F32),
        compiler_params=pltpu.CompilerParams(dimension_semantics=("arbitrary", "arbitrary"),
                                             vmem_limit_bytes=VMEM_LIMIT),
        name="ada_mod",
    )(conds, ada_w, ada_b.reshape(depth, 1, n6))


def _proj_kernel(x_ref, mod_ref, g_ref, wm_ref, wg_ref, pm_ref, pg_ref):
    x = x_ref[...]
    y = x * lax.rsqrt(jnp.mean(x * x, axis=-1, keepdims=True) + EPS)
    h = (y * g_ref[...]) * (1.0 + mod_ref[0, 1:2, :]) + mod_ref[0, 0:1, :]
    hb = h.astype(BF16)
    pm_ref[...] = jnp.dot(hb, wm_ref[...], preferred_element_type=F32)
    pg_ref[...] = jnp.dot(hb, wg_ref[...], preferred_element_type=F32)


def _mod_map(n_ctx_tiles, tiles_per_latent_seq):
    def index(i, *_):
        return (jnp.where(i < n_ctx_tiles, 0, 1 + (i - n_ctx_tiles) // tiles_per_latent_seq), 0, 0)
    return index


def _proj(x, mod, g, w_main, w_gate, mod_map, tm):
    t, d = x.shape
    const = lambda i: (0, 0)
    return pl.pallas_call(
        _proj_kernel,
        grid=(t // tm,),
        in_specs=[pl.BlockSpec((tm, d), lambda i: (i, 0)),
                  pl.BlockSpec((1, 8, d), mod_map),
                  pl.BlockSpec((1, d), const),
                  pl.BlockSpec(w_main.shape, const),
                  pl.BlockSpec(w_gate.shape, const)],
        out_specs=[pl.BlockSpec((tm, P_MAIN), lambda i: (i, 0)),
                   pl.BlockSpec((tm, w_gate.shape[1]), lambda i: (i, 0))],
        out_shape=[jax.ShapeDtypeStruct((t, P_MAIN), F32),
                   jax.ShapeDtypeStruct((t, w_gate.shape[1]), F32)],
        compiler_params=pltpu.CompilerParams(dimension_semantics=("arbitrary",),
                                             vmem_limit_bytes=VMEM_LIMIT),
        name="in_proj",
    )(x, mod, g, w_main, w_gate)


def _mixer_direction(d, c_idx, n_chunks, use_rope, pm, hprev, hnext, cos_ref, sin_ref,
                     small_b, a_row, ret_rep, wg, gb, conv_w, conv_b, d_row, spread_b, spread_dt,
                     o_ref, mS, mm, rS, gS, sS, lat_s, cst_s):
    C = CHUNK
    last = C - 1 if d == 0 else 0
    row = lax.broadcasted_iota(jnp.int32, (C, C), 0)
    col = lax.broadcasted_iota(jnp.int32, (C, C), 1)
    mask = (col <= row) if d == 0 else (col >= row)
    mask_t = (row <= col) if d == 0 else (row >= col)
    tri = mask.astype(BF16)
    tri_t_ones = jnp.concatenate([mask_t.astype(BF16), jnp.ones((C, C), BF16)], axis=1)
    neg_inf = jnp.float32(-jnp.inf)
    lane = lax.broadcasted_iota(jnp.int32, (1, LANES), 1)
    even = lane < 64
    pair_sel = (even, jnp.logical_not(even))
    rows_of = (slice(0, 64), slice(64, 128))

    small = pm[:, SMALL:SMALL + LANES]
    pre = small + small_b[...]
    sp = _softplus(pre)
    lsg = -_softplus(-pre)
    a_neg = -jnp.exp(a_row[...])
    la = jnp.where(lane < S_MF, pre,
                   jnp.where(lane < S_DT, lsg,
                             jnp.where(lane < S_DTRAW, sp * a_neg, 0.0)))
    la_t = la.T
    lat_s[d] = la_t
    cst_s[d] = _dot_mask(la_t, tri_t_ones)
    cs = _mask_dot(tri, la)
    b_spread = _dot_spread(cs, spread_b[d])
    dt_spread = _dot_spread(sp, spread_dt[d])

    q_all = pm[:, MQ:MQ + 256] * (64.0 ** -0.5)
    k_all = pm[:, MK:MK + 256]
    v_all = pm[:, MV:MV + 256]
    k_t = k_all.T
    ones_blk = jnp.ones((C, LANES), BF16)
    for p in range(2):
        ps = slice(128 * p, 128 * p + 128)
        q_pair = q_all[:, ps].astype(BF16)
        k_pair = k_all[:, ps]
        vx = jnp.concatenate([v_all[:, ps].astype(BF16), ones_blk], axis=1)
        outs = []
        for r in range(2):
            h = 2 * p + r
            ci = d * 4 + h
            b_full = b_spread[:, 128 * h:128 * h + 128]
            b_row = cst_s[d, S_MF + ci:S_MF + ci + 1, 0:C]
            bl_row = cst_s[d, S_MF + ci:S_MF + ci + 1, C:2 * C]
            u_row = lat_s[d, S_MI + ci:S_MI + ci + 1, :] - b_row
            ms_row = mm[0, ci:ci + 1, :]
            d_log = jnp.where(mask, b_full + u_row, neg_inf)
            inter = b_full + ms_row
            m_t = jnp.maximum(inter, jnp.max(d_log, axis=1, keepdims=True))
            w = jnp.exp(d_log - m_t)
            k_m = jnp.where(pair_sel[r], k_pair, 0.0)
            sc = _dot_nt(q_pair, k_m) * w
            w_inter = jnp.exp(inter - m_t)
            s_old = mS[0, d, h]
            intra = _dot(sc, vx)
            cross = _dot(q_pair, s_old)
            num = intra[:, 0:LANES] + w_inter * cross[:, 0:LANES]
            den = intra[:, LANES:] + w_inter * cross[:, LANES:]
            outs.append(num / jnp.maximum(jnp.abs(den), jnp.exp(-m_t)))
            g_row = bl_row + u_row
            m_new = jnp.maximum(bl_row + ms_row, jnp.max(g_row, axis=1, keepdims=True))
            wk_row = jnp.exp(g_row - m_new)
            decay = jnp.exp(bl_row + ms_row - m_new)
            upd = _dot(k_t[64 * h:64 * h + 64, :] * wk_row, vx)
            live = rows_of[r]
            mS[0, d, h, live, 0:LANES] = decay * s_old[live, 0:LANES] + upd[:, 0:LANES]
            mS[0, d, h, live, LANES:] = decay * s_old[live, LANES:] + upd[:, LANES:]
            mm[0, ci:ci + 1, :] = m_new
        o_ref[:, ps] = jnp.where(even, outs[0], outs[1])

    rq = pm[:, RQ:RQ + 256]
    rk = pm[:, RK:RK + 256] * (64.0 ** -0.5)
    rv = pm[:, RV:RV + 256]
    if use_rope:
        cosv = cos_ref[...]
        sinv = sin_ref[...]
        first = (lax.broadcasted_iota(jnp.int32, (1, 256), 1) % 64) < 32

        def rope(x):
            rot = jnp.where(first, pltpu.roll(x, 256 - 32, 1), pltpu.roll(x, 32, 1))
            return x * cosv + rot * sinv

        rq = rope(rq)
        rk = rope(rk)
    rk_t = rk.T
    adiff = jnp.abs(row - col).astype(F32)
    ri_f = lax.broadcasted_iota(jnp.int32, (C, LANES), 0).astype(F32)
    cj_f = lax.broadcasted_iota(jnp.int32, (1, C), 1).astype(F32)
    ret_la = -jnp.exp(ret_rep[...])
    for p in range(2):
        ps = slice(128 * p, 128 * p + 128)
        q_pair = rq[:, ps].astype(BF16)
        k_pair = rk[:, ps]
        v_pair = rv[:, ps].astype(BF16)
        outs = []
        for r in range(2):
            h = 2 * p + r
            ci = d * 4 + h
            la_h = ret_la[ci:ci + 1, :]
            dec = jnp.where(mask, jnp.exp(la_h * adiff), 0.0)
            if d == 0:
                eb = jnp.exp(la_h * (ri_f + 1.0))
                kw_row = jnp.exp(la_h * (float(C - 1) - cj_f))
            else:
                eb = jnp.exp(la_h * (float(C) - ri_f))
                kw_row = jnp.exp(la_h * cj_f)
            k_m = jnp.where(pair_sel[r], k_pair, 0.0)
            sc = _dot_nt(q_pair, k_m) * dec
            s_old = rS[0, d, h]
            outs.append(_dot(sc, v_pair) + eb * _dot(q_pair, s_old))
            live = rows_of[r]
            rS[0, d, h, live, :] = (jnp.exp(la_h * float(C)) * s_old[live, :]
                                    + _dot(rk_t[64 * h:64 * h + 64, :] * kw_row, v_pair))
        o_ref[:, 256 + 128 * p:256 + 128 * p + 128] = jnp.where(even, outs[0], outs[1])

    gate = _dot(small, wg[d]) + gb[d]
    lga = (-_softplus(-gate)) * (1.0 / GLA_TAU)
    b = _mask_dot(tri, lga)
    bl_row = b[last:last + 1, :]
    b_mid = b[C // 2:C // 2 + 1, :]
    gq = pm[:, GQ:GQ + 128]
    gk = pm[:, GK:GK + 128] * (32.0 ** -0.5)
    gv = pm[:, GV:GV + 256]
    qc = (gq * jnp.exp(b - b_mid)).astype(BF16)
    kc = gk * jnp.exp(b_mid - b)
    qe = (gq * jnp.exp(b)).astype(BF16)
    kd = gk * jnp.exp(bl_row - b)
    ebl = jnp.exp(bl_row)
    gv_t = gv.T
    key_head = lane // 32
    for p in range(2):
        v_pair = gv[:, 128 * p:128 * p + 128].astype(BF16)
        outs = []
        for r in range(2):
            h = 2 * p + r
            mine = key_head == h
            sc = jnp.where(mask, _dot_nt(qc, jnp.where(mine, kc, 0.0)), 0.0)
            st_old = gS[0, d, h]
            outs.append(_dot(sc, v_pair) + _dot_nt(qe, st_old))
            live = rows_of[r]
            gS[0, d, h, live, :] = (st_old[live, :] * ebl
                                    + _dot(gv_t[64 * h:64 * h + 64, :], jnp.where(mine, kd, 0.0)))
        o_ref[:, 512 + 128 * p:512 + 128 * p + 128] = jnp.where(even, outs[0], outs[1])

    xr = pm[:, SXBC:SXBC + 512]
    prev = jnp.where(c_idx > 0, hprev[SUBLANES - 1:SUBLANES, :], 0.0)
    nxt = jnp.where(c_idx < n_chunks - 1, hnext[0:1, :], 0.0)
    rowi = lax.broadcasted_iota(jnp.int32, (C, 512), 0)
    x_dn = jnp.where(rowi == 0, prev, pltpu.roll(xr, 1, 0))
    x_up = jnp.where(rowi == C - 1, nxt, pltpu.roll(xr, C - 1, 0))
    y = conv_w[0:1, :] * x_dn + conv_w[1:2, :] * xr + conv_w[2:3, :] * x_up + conv_b[...]
    xbc = y * _sigmoid(y)
    sx = xbc[:, 0:256]
    s_b = xbc[:, 256:384]
    s_c = xbc[:, 384:512].astype(BF16)
    sb_t = s_b.T
    xdt = sx * dt_spread
    for p in range(2):
        ps = slice(128 * p, 128 * p + 128)
        v_pair = xdt[:, ps].astype(BF16)
        raw = _dot_nt(s_c, jnp.where(pair_sel[p], s_b, 0.0))
        outs = []
        for r in range(2):
            h = 2 * p + r
            ci = d * 4 + h
            b_full = b_spread[:, 512 + 128 * h:512 + 128 * h + 128]
            b_row = cst_s[d, S_DT + ci:S_DT + ci + 1, 0:C]
            bl_row = cst_s[d, S_DT + ci:S_DT + ci + 1, C:2 * C]
            dec = jnp.exp(jnp.where(mask, b_full - b_row, neg_inf))
            s_old = sS[0, d, h]
            outs.append(_dot(raw * dec, v_pair) + jnp.exp(b_full) * _dot(s_c, s_old))
            live = rows_of[p]
            sS[0, d, h, live, :] = (jnp.exp(bl_row) * s_old[live, :]
                                    + _dot(sb_t[64 * p:64 * p + 64, :] * jnp.exp(bl_row - b_row), v_pair))
        o = jnp.where(even, outs[0], outs[1])
        if d == 0:
            o = o + d_row[:, ps] * sx[:, ps]
        o_ref[:, 768 + 128 * p:768 + 128 * p + 128] = o


def _mixer_kernel(n_chunks, use_rope, *refs):
    it = iter(refs)
    pm_f, pm_b, hpf, hnf, hpb, hnb = (next(it) for _ in range(6))
    if use_rope:
        cos_f, sin_f, cos_b, sin_b = (next(it) for _ in range(4))
    else:
        cos_f = sin_f = cos_b = sin_b = None
    mS0, mm0, r0, g0, s0 = (next(it) for _ in range(5))
    params = tuple(next(it) for _ in range(10))
    o_f, o_b, mS, mm, rS, gS, sS = (next(it) for _ in range(7))
    lat_s, cst_s = (next(it) for _ in range(2))

    s = pl.program_id(1)

    @pl.when(s == 0)
    def _():
        mS[...] = mS0[...]
        mm[...] = mm0[...]
        rS[...] = r0[...]
        gS[...] = g0[...]
        sS[...] = s0[...]

    _mixer_direction(0, s, n_chunks, use_rope, pm_f, hpf, hnf, cos_f, sin_f, *params,
                     o_f, mS, mm, rS, gS, sS, lat_s, cst_s)
    _mixer_direction(1, n_chunks - 1 - s, n_chunks, use_rope, pm_b, hpb, hnb, cos_b, sin_b, *params,
                     o_b, mS, mm, rS, gS, sS, lat_s, cst_s)


def _mixers(pm, init, params, rope, n_seq, seq_len, row0):
    C = CHUNK
    n = seq_len // C
    t = pm.shape[0]
    chunk0 = row0 // C
    rb = C // SUBLANES
    n_rb = t // SUBLANES
    xbc_blk = SXBC // 512
    use_rope = rope is not None

    def f_idx(b, s):
        return chunk0 + b * n + s

    def b_idx(b, s):
        return chunk0 + b * n + (n - 1 - s)

    in_specs = [
        pl.BlockSpec((C, P_MAIN), lambda b, s: (f_idx(b, s), 0)),
        pl.BlockSpec((C, P_MAIN), lambda b, s: (b_idx(b, s), 0)),
        pl.BlockSpec((SUBLANES, 512), lambda b, s: (jnp.maximum(f_idx(b, s) * rb - 1, 0), xbc_blk)),
        pl.BlockSpec((SUBLANES, 512), lambda b, s: (jnp.minimum(f_idx(b, s) * rb + rb, n_rb - 1), xbc_blk)),
        pl.BlockSpec((SUBLANES, 512), lambda b, s: (jnp.maximum(b_idx(b, s) * rb - 1, 0), xbc_blk)),
        pl.BlockSpec((SUBLANES, 512), lambda b, s: (jnp.minimum(b_idx(b, s) * rb + rb, n_rb - 1), xbc_blk)),
    ]
    args = [pm, pm, pm, pm, pm, pm]
    if use_rope:
        cos, sin = rope
        in_specs += [pl.BlockSpec((C, 256), lambda b, s: (s, 0)),
                     pl.BlockSpec((C, 256), lambda b, s: (s, 0)),
                     pl.BlockSpec((C, 256), lambda b, s: (n - 1 - s, 0)),
                     pl.BlockSpec((C, 256), lambda b, s: (n - 1 - s, 0))]
        args += [cos, sin, cos, sin]

    def state_spec(shape):
        blk = (1,) + tuple(shape[1:])
        nd = len(shape)
        return pl.BlockSpec(blk, lambda b, s: (b,) + (0,) * (nd - 1))

    for a in init:
        in_specs.append(state_spec(a.shape))
        args.append(a)
    for p in params:
        nd = p.ndim
        in_specs.append(pl.BlockSpec(p.shape, lambda b, s, nd=nd: (0,) * nd))
        args.append(p)

    t_out = n_seq * seq_len
    out_specs = [pl.BlockSpec((C, 1024), lambda b, s: (b * n + s, 0)),
                 pl.BlockSpec((C, 1024), lambda b, s: (b * n + (n - 1 - s), 0))]
    out_shape = [jax.ShapeDtypeStruct((t_out, 1024), F32), jax.ShapeDtypeStruct((t_out, 1024), F32)]
    for a in init:
        out_specs.append(state_spec(a.shape))
        out_shape.append(jax.ShapeDtypeStruct(a.shape, F32))

    scratch = [pltpu.VMEM((2, LANES, C), F32), pltpu.VMEM((2, LANES, 2 * C), F32)]
    return pl.pallas_call(
        functools.partial(_mixer_kernel, n, use_rope),
        grid=(n_seq, n),
        in_specs=in_specs,
        out_specs=out_specs,
        out_shape=out_shape,
        scratch_shapes=scratch,
        compiler_params=pltpu.CompilerParams(dimension_semantics=("arbitrary", "arbitrary"),
                                             vmem_limit_bytes=VMEM_LIMIT),
        name="mixers_rope" if use_rope else "mixers",
    )(*args)


def _post_kernel(n_experts, n_ctx_tiles, ofc_ref, obc_ref, ofl_ref, obl_ref, pg_ref, x_ref, mod_ref,
                 hg_ref, bd_ref, wo_ref, n2_ref, rw_ref, rb_ref, ltri_ref,
                 x1_ref, h2_ref, ei_ref, gt_ref, rk_ref, cnt_ref):
    @pl.when(pl.program_id(0) == 0)
    def _():
        cnt_ref[...] = jnp.zeros_like(cnt_ref)

    y = jnp.where(pl.program_id(0) < n_ctx_tiles, ofc_ref[...] + obc_ref[...], ofl_ref[...] + obl_ref[...])
    gt = pg_ref[...]
    lane = lax.broadcasted_iota(jnp.int32, (1, y.shape[1]), 1)
    sig = _sigmoid(gt)
    silu = gt * sig
    pre = y * jnp.where(lane >= 768, silu, 1.0)
    sq = (pre * pre).astype(BF16)
    ms = jnp.concatenate([jnp.dot(sq[:, LANES * s:LANES * s + LANES], bd_ref[...], preferred_element_type=F32)
                          for s in range(y.shape[1] // LANES)], axis=1) * (1.0 / 64.0)
    nrm = pre * lax.rsqrt(ms + EPS) * hg_ref[...]
    post = jnp.where(lane < 256, sig, jnp.where(lane < 768, silu, 1.0))
    mixed = jnp.dot((nrm * post).astype(BF16), wo_ref[...], preferred_element_type=F32)
    x1 = x_ref[...] + mod_ref[0, 2:3, :] * mixed
    x1_ref[...] = x1
    yn = x1 * lax.rsqrt(jnp.mean(x1 * x1, axis=-1, keepdims=True) + EPS)
    h2 = (yn * n2_ref[...]) * (1.0 + mod_ref[0, 4:5, :]) + mod_ref[0, 3:4, :]
    for s in range(h2_ref.shape[0]):
        h2_ref[s] = h2[:, LANES * s:LANES * s + LANES]

    tm = h2.shape[0]
    lane_f = lax.broadcasted_iota(jnp.int32, (1, LANES), 1).astype(F32)
    neg_inf = jnp.float32(-jnp.inf)
    logits = _dot(h2, rw_ref[...]) + rb_ref[...]
    work = jnp.where(lane_f < float(n_experts), logits, neg_inf)
    vals, idxs, hots = [], [], []
    for _ in range(TOP_K):
        m = jnp.max(work, axis=1, keepdims=True)
        idx = jnp.min(jnp.where(work == m, lane_f, float(LANES)), axis=1, keepdims=True)
        hot = lane_f == idx
        vals.append(m)
        idxs.append(idx)
        hots.append(hot)
        work = jnp.where(hot, neg_inf, work)
    exps = [jnp.exp(v - vals[0]) for v in vals]
    denom = exps[0]
    for e in exps[1:]:
        denom = denom + e
    sel = hots[0].astype(F32)
    for hot in hots[1:]:
        sel = sel + hot.astype(F32)
    before = jnp.dot(ltri_ref[...], sel.astype(BF16), preferred_element_type=F32) + cnt_ref[...]
    ranks = [jnp.sum(jnp.where(hot, before, 0.0), axis=1, keepdims=True) for hot in hots]
    cnt_ref[...] = cnt_ref[...] + jnp.sum(sel, axis=0, keepdims=True)
    slot = lax.broadcasted_iota(jnp.int32, (tm, TOP_K), 1)

    def pack(cols):
        out = cols[TOP_K - 1]
        for k in range(TOP_K - 2, -1, -1):
            out = jnp.where(slot == k, cols[k], out)
        return out

    ei_ref[...] = pack(idxs).astype(jnp.int32)
    gt_ref[...] = pack([e / denom for e in exps])
    rk_ref[...] = pack(ranks).astype(jnp.int32)


def _post(o_ctx, o_lat, pg, x, mod, hg, bd, w_out, n2g, rw, rb, ltri, n_experts, mod_map, tm):
    t, d = x.shape
    n_slab = d // LANES
    nct = o_ctx[0].shape[0] // tm
    row = lambda i: (i, 0)
    const = lambda i: (0, 0)
    ctx_row = lambda i: (jnp.minimum(i, nct - 1), 0)
    lat_row = lambda i: (jnp.maximum(i - nct, 0), 0)
    return pl.pallas_call(
        functools.partial(_post_kernel, n_experts, nct),
        grid=(t // tm,),
        in_specs=[pl.BlockSpec((tm, d), ctx_row), pl.BlockSpec((tm, d), ctx_row),
                  pl.BlockSpec((tm, d), lat_row), pl.BlockSpec((tm, d), lat_row),
                  pl.BlockSpec((tm, d), row),
                  pl.BlockSpec((tm, d), row), pl.BlockSpec((1, 8, d), mod_map),
                  pl.BlockSpec((1, d), const), pl.BlockSpec((LANES, LANES), const), pl.BlockSpec((d, d), const),
                  pl.BlockSpec((1, d), const), pl.BlockSpec((d, LANES), const), pl.BlockSpec((1, LANES), const),
                  pl.BlockSpec((tm, tm), const)],
        out_specs=[pl.BlockSpec((tm, d), row), pl.BlockSpec((n_slab, tm, LANES), lambda i: (0, i, 0)),
                   pl.BlockSpec((tm, TOP_K), row), pl.BlockSpec((tm, TOP_K), row),
                   pl.BlockSpec((tm, TOP_K), row), pl.BlockSpec((1, LANES), const)],
        out_shape=[jax.ShapeDtypeStruct((t, d), F32), jax.ShapeDtypeStruct((n_slab, t, LANES), F32),
                   jax.ShapeDtypeStruct((t, TOP_K), jnp.int32), jax.ShapeDtypeStruct((t, TOP_K), F32),
                   jax.ShapeDtypeStruct((t, TOP_K), jnp.int32), jax.ShapeDtypeStruct((1, LANES), F32)],
        compiler_params=pltpu.CompilerParams(dimension_semantics=("arbitrary",),
                                             vmem_limit_bytes=VMEM_LIMIT),
        name="post_mix",
    )(o_ctx[0], o_ctx[1], o_lat[0], o_lat[1], pg, x, mod, hg, bd, w_out, n2g, rw, rb, ltri)


def _dispatch_kernel(tm, dest_ref, h_ref, xb_in, xb_out, sem):
    del xb_in

    def row_copy(t, k):
        return pltpu.make_async_copy(h_ref.at[:, pl.ds(t, 1), :],
                                     xb_out.at[:, pl.ds(dest_ref[t * TOP_K + k], 1), :], sem)

    def start(t, c):
        for k in range(TOP_K):
            row_copy(t, k).start(priority=k % 2)
        return c

    def wait(t, c):
        for k in range(TOP_K):
            row_copy(t, k).wait()
        return c

    lax.fori_loop(0, tm, start, 0, unroll=8)
    lax.fori_loop(0, tm, wait, 0, unroll=8)


def _dispatch(h2, dest, xb_zero, tm):
    n_slab, t, _ = h2.shape
    return pl.pallas_call(
        functools.partial(_dispatch_kernel, tm),
        grid=(t // tm,),
        in_specs=[pl.BlockSpec((tm * TOP_K,), lambda i: (i,), memory_space=pltpu.SMEM),
                  pl.BlockSpec((n_slab, tm, LANES), lambda i: (0, i, 0)),
                  pl.BlockSpec(memory_space=pl.ANY)],
        out_specs=pl.BlockSpec(memory_space=pl.ANY),
        out_shape=jax.ShapeDtypeStruct(xb_zero.shape, xb_zero.dtype),
        scratch_shapes=[pltpu.SemaphoreType.DMA(())],
        input_output_aliases={2: 0},
        compiler_params=pltpu.CompilerParams(dimension_semantics=("arbitrary",),
                                             vmem_limit_bytes=VMEM_LIMIT),
        name="moe_dispatch",
    )(dest, h2, xb_zero)


def _moe_kernel(be_ref, nu_ref, xb_ref, wgu_ref, bgu_ref, wdn_ref, bdn_ref, o_ref, wgu_b, wdn_b):
    i = pl.program_id(0)
    d_ff = wdn_ref.shape[2]
    n_slab = xb_ref.shape[0]
    active = i < nu_ref[0]
    new_expert = jnp.logical_or(i == 0, be_ref[i] != be_ref[jnp.maximum(i - 1, 0)])

    @pl.when(jnp.logical_and(active, new_expert))
    def _():
        wgu_b[...] = wgu_ref[0, 0].astype(BF16)
        wdn_b[...] = wdn_ref[0, 0].astype(BF16)

    @pl.when(active)
    def _():
        x = jnp.concatenate([xb_ref[s].astype(BF16) for s in range(n_slab)], axis=1)
        gu = jnp.dot(x, wgu_b[...], preferred_element_type=F32) + bgu_ref[0, 0]
        gate = jnp.minimum(gu[:, :d_ff], SWIGLU_LIMIT)
        up = jnp.clip(gu[:, d_ff:], -SWIGLU_LIMIT, SWIGLU_LIMIT)
        act = gate * _sigmoid(SWIGLU_ALPHA * gate) * (up + 1.0)
        out = jnp.dot(act.astype(BF16), wdn_b[...], preferred_element_type=F32) + bdn_ref[0, 0]
        for s in range(n_slab):
            o_ref[s] = out[:, LANES * s:LANES * s + LANES]

    @pl.when(jnp.logical_not(active))
    def _():
        o_ref[...] = jnp.zeros_like(o_ref)


def _moe(xb, block_e, n_used, w_gu, b_gu, w_dn, b_dn, layer, bm):
    n_slab, rows, _ = xb.shape
    depth, n_e, d, n_gu = w_gu.shape
    d_ff = w_dn.shape[2]
    grid_spec = pltpu.PrefetchScalarGridSpec(
        num_scalar_prefetch=2,
        grid=(rows // bm,),
        in_specs=[pl.BlockSpec((n_slab, bm, LANES), lambda i, be, nu: (0, i, 0)),
                  pl.BlockSpec((1, 1, d, n_gu), lambda i, be, nu: (layer, be[i], 0, 0)),
                  pl.BlockSpec((1, 1, 1, n_gu), lambda i, be, nu: (layer, be[i], 0, 0)),
                  pl.BlockSpec((1, 1, d_ff, d), lambda i, be, nu: (layer, be[i], 0, 0)),
                  pl.BlockSpec((1, 1, 1, d), lambda i, be, nu: (layer, be[i], 0, 0))],
        out_specs=pl.BlockSpec((n_slab, bm, LANES), lambda i, be, nu: (0, i, 0)),
        scratch_shapes=[pltpu.VMEM((d, n_gu), BF16), pltpu.VMEM((d_ff, d), BF16)],
    )
    return pl.pallas_call(
        _moe_kernel,
        grid_spec=grid_spec,
        out_shape=jax.ShapeDtypeStruct((n_slab, rows, LANES), F32),
        compiler_params=pltpu.CompilerParams(dimension_semantics=("arbitrary",),
                                             vmem_limit_bytes=VMEM_LIMIT),
        name="moe_ffn",
    )(block_e, n_used, xb, w_gu, b_gu.reshape(depth, n_e, 1, n_gu), w_dn, b_dn.reshape(depth, n_e, 1, d))


def _combine_kernel(final, tm, dest_ref, x1_ref, gt_ref, mod_ref, fg_ref, yb_ref, o_ref, ybuf, sem):
    def row_copy(t, k):
        return pltpu.make_async_copy(yb_ref.at[:, pl.ds(dest_ref[t * TOP_K + k], 1), :],
                                     ybuf.at[k, :, pl.ds(t, 1), :], sem)

    def start(t, c):
        for k in range(TOP_K):
            row_copy(t, k).start(priority=k % 2)
        return c

    def wait(t, c):
        for k in range(TOP_K):
            row_copy(t, k).wait()
        return c

    lax.fori_loop(0, tm, start, 0, unroll=8)
    lax.fori_loop(0, tm, wait, 0, unroll=8)

    n_slab = ybuf.shape[1]
    gt = gt_ref[...]
    g_full = [jnp.broadcast_to(gt[:, k:k + 1], (tm, LANES)) for k in range(TOP_K)]
    slabs = []
    for s in range(n_slab):
        cs = slice(LANES * s, LANES * s + LANES)
        y = g_full[0] * ybuf[0, s]
        for k in range(1, TOP_K):
            y = y + g_full[k] * ybuf[k, s]
        slabs.append(x1_ref[:, cs] + mod_ref[0, 5:6, cs] * y)
    x2 = jnp.concatenate(slabs, axis=1)
    if final:
        x2 = x2 * lax.rsqrt(jnp.mean(x2 * x2, axis=-1, keepdims=True) + EPS) * fg_ref[...]
    o_ref[...] = x2


def _combine(x1, yb, dest, gates, mod, final_g, final, mod_map, tm):
    t, d = x1.shape
    n_slab = d // LANES
    return pl.pallas_call(
        functools.partial(_combine_kernel, final, tm),
        grid=(t // tm,),
        in_specs=[pl.BlockSpec((tm * TOP_K,), lambda i: (i,), memory_space=pltpu.SMEM),
                  pl.BlockSpec((tm, d), lambda i: (i, 0)),
                  pl.BlockSpec((tm, TOP_K), lambda i: (i, 0)),
                  pl.BlockSpec((1, 8, d), mod_map),
                  pl.BlockSpec((1, d), lambda i: (0, 0)),
                  pl.BlockSpec(memory_space=pl.ANY)],
        out_specs=pl.BlockSpec((tm, d), lambda i: (i, 0)),
        out_shape=jax.ShapeDtypeStruct((t, d), F32),
        scratch_shapes=[pltpu.VMEM((TOP_K, n_slab, tm, LANES), F32), pltpu.SemaphoreType.DMA(())],
        compiler_params=pltpu.CompilerParams(dimension_semantics=("arbitrary",),
                                             vmem_limit_bytes=VMEM_LIMIT),
        name="moe_combine",
    )(dest, x1, gates, mod, final_g, yb)


def _rope_tables(n_tokens):
    rows = n_tokens // GRID_W
    r, c = jnp.meshgrid(jnp.arange(rows), jnp.arange(GRID_W), indexing='ij')
    nf = 16
    inv = ROPE_BASE ** (-jnp.arange(nf, dtype=F32) / nf)
    ang = jnp.concatenate([r.reshape(-1, 1).astype(F32) * inv, c.reshape(-1, 1).astype(F32) * inv], axis=-1)
    cos, sin = jnp.cos(ang), jnp.sin(ang)
    cos_t = jnp.tile(jnp.concatenate([cos, cos], axis=-1), (1, 4))
    sin_t = jnp.tile(jnp.concatenate([-sin, sin], axis=-1), (1, 4))
    return cos_t, sin_t


def _pad_lanes(a, width=LANES):
    return jnp.pad(a, [(0, 0)] * (a.ndim - 1) + [(0, width - a.shape[-1])])


def _spread_matrices():
    sb = np.zeros((2, LANES, 1024), np.float32)
    sdt = np.zeros((2, LANES, 256), np.float32)
    for d in range(2):
        for h in range(4):
            ci = d * 4 + h
            sb[d, S_MF + ci, 128 * h:128 * h + 128] = 1.0
            sb[d, S_DT + ci, 512 + 128 * h:512 + 128 * h + 128] = 1.0
            sdt[d, S_DT + ci, 64 * h:64 * h + 64] = 1.0
    return jnp.asarray(sb, dtype=BF16), jnp.asarray(sdt, dtype=BF16)


def _pad_states(c_m, n_m, m_m, s_ret, s_gla, s_ssd):
    bsz = c_m.shape[0]
    mS = jnp.zeros((bsz, 2, 4, 128, 256), F32)
    rS = jnp.zeros((bsz, 2, 4, 128, 128), F32)
    gS = jnp.zeros((bsz, 2, 4, 128, 128), F32)
    sS = jnp.zeros((bsz, 2, 4, 128, 128), F32)
    for h in range(4):
        r, g = h % 2, h // 2
        lv = slice(64 * r, 64 * r + 64)
        mS = mS.at[:, :, h, lv, lv].set(c_m[:, :, h])
        mS = mS.at[:, :, h, lv, 128:].set(jnp.broadcast_to(n_m[:, :, h, :, None], (bsz, 2, 64, 128)))
        rS = rS.at[:, :, h, lv, lv].set(s_ret[:, :, h])
        gS = gS.at[:, :, h, lv, 32 * h:32 * h + 32].set(jnp.swapaxes(s_gla[:, :, h], -1, -2))
        sS = sS.at[:, :, h, 64 * g:64 * g + 64, lv].set(s_ssd[:, :, h])
    mm = jnp.broadcast_to(m_m.reshape(bsz, 8, 1), (bsz, 8, LANES))
    return mS, mm, rS, gS, sS


def _unpad_states(mS, mm, rS, gS, sS):
    bsz = mS.shape[0]
    c_m, n_m, s_ret, s_gla, s_ssd = [], [], [], [], []
    for h in range(4):
        r, g = h % 2, h // 2
        lv = slice(64 * r, 64 * r + 64)
        c_m.append(mS[:, :, h, lv, lv])
        n_m.append(mS[:, :, h, lv, 128])
        s_ret.append(rS[:, :, h, lv, lv])
        s_gla.append(jnp.swapaxes(gS[:, :, h, lv, 32 * h:32 * h + 32], -1, -2))
        s_ssd.append(sS[:, :, h, 64 * g:64 * g + 64, lv])
    stack = lambda xs: jnp.stack(xs, axis=2)
    return (stack(c_m), stack(n_m), mm[:, :, 0].reshape(bsz, 2, 4), stack(s_ret), stack(s_gla), stack(s_ssd))


def kernel(x_prompt, x_sample, c, state_mlstm_C, state_mlstm_n, state_mlstm_m, state_ret, state_gla, state_ssd,
           c_ctx, ada_w, ada_b, norm1_g, norm2_g, final_g, w_in, w_out,
           mlstm_i_b, mlstm_f_b, mlstm_norm_g, ret_decay, ret_norm_g,
           gla_gate_w, gla_gate_b, gla_norm_g,
           ssd_conv_w, ssd_conv_b, ssd_dt_b, ssd_A_log, ssd_D, ssd_norm_g,
           router_w, router_b, moe_w_gu, moe_b_gu, moe_w_down, moe_b_down):
    depth = w_in.shape[0]
    bp, lp, d = x_prompt.shape
    bs, ls, _ = x_sample.shape
    n_e = router_w.shape[-1]
    tm = 256
    bm = 512

    offs = np.cumsum([0, 256, 256, 256, 256, 8, 8, 256, 256, 256, 256, 128, 128, 256, 256, 32, 256, 512, 8])
    (o_mq, o_mk, o_mv, o_mo, o_mi, o_mf, o_rq, o_rk, o_rv, o_rg, o_gq, o_gk, o_gv, o_gr, o_glr,
     o_sz, o_sxbc, o_sdt, _) = [int(v) for v in offs]

    def cols(a, n):
        return w_in[:, :, a:a + n]

    small_w = jnp.concatenate([cols(o_mi, 8), cols(o_mf, 8), cols(o_sdt, 8),
                               jnp.zeros((depth, d, 8), F32), cols(o_glr, 32),
                               jnp.zeros((depth, d, LANES - 64), F32)], axis=-1)
    w_main = jnp.concatenate([cols(o_mq, 256), cols(o_mk, 256), cols(o_mv, 256),
                              cols(o_rq, 256), cols(o_rk, 256), cols(o_rv, 256),
                              cols(o_gq, 128), cols(o_gk, 128), cols(o_gv, 256),
                              cols(o_sxbc, 512), small_w], axis=-1).astype(BF16)
    w_gate = jnp.concatenate([cols(o_mo, 256), cols(o_rg, 256), cols(o_gr, 256), cols(o_sz, 256)],
                             axis=-1).astype(BF16)
    w_out_b = w_out.astype(BF16)
    rw = _pad_lanes(router_w).astype(BF16)
    rb = _pad_lanes(router_b)[:, None, :]
    hg = jnp.concatenate([mlstm_norm_g, ret_norm_g, gla_norm_g, ssd_norm_g], axis=-1)[:, None, :]
    gidx = np.arange(LANES) // 64
    bd = jnp.asarray(gidx[:, None] == gidx[None, :], dtype=BF16)

    small_b = _pad_lanes(jnp.concatenate([mlstm_i_b.reshape(depth, 8), mlstm_f_b.reshape(depth, 8),
                                          ssd_dt_b.reshape(depth, 8)], axis=-1))[:, None, :]
    a_row = _pad_lanes(jnp.concatenate([jnp.zeros((depth, 16), F32), ssd_A_log.reshape(depth, 8)],
                                       axis=-1))[:, None, :]
    ret_rep = jnp.broadcast_to(ret_decay.reshape(depth, 8, 1), (depth, 8, LANES))
    wg = jnp.zeros((depth, 2, LANES, LANES), F32)
    wg = wg.at[:, 0, S_GLR:S_GLR + 16, :].set(gla_gate_w[:, 0])
    wg = wg.at[:, 1, S_GLR + 16:S_GLR + 32, :].set(gla_gate_w[:, 1])
    wg = wg.astype(BF16)
    gb = gla_gate_b[:, :, None, :]
    d_row = jnp.repeat(ssd_D, 64, axis=-1)[:, None, :]
    conv_b = ssd_conv_b[:, None, :]
    spread_b, spread_dt = _spread_matrices()

    n_cond = 1 + bs
    pad_rows = (-n_cond) % 8
    conds = jnp.concatenate([c_ctx[None, :], c, jnp.zeros((pad_rows, d), F32)], axis=0)
    mod_all = _ada_mod(conds, ada_w, ada_b).reshape(depth, n_cond + pad_rows, 6, d)
    mod_all = jnp.pad(mod_all, ((0, 0), (0, 0), (0, 2), (0, 0)))

    rope = _rope_tables(ls)
    zero = lambda *s: jnp.zeros(s, F32)
    init_ctx = (zero(bp, 2, 4, 128, 256), zero(bp, 8, LANES), zero(bp, 2, 4, 128, 128),
                zero(bp, 2, 4, 128, 128), zero(bp, 2, 4, 128, 128))

    t_ctx = bp * lp
    t_tot = t_ctx + bs * ls
    x = jnp.concatenate([x_prompt.reshape(t_ctx, d), x_sample.reshape(bs * ls, d)], axis=0)
    mod_map = _mod_map(t_ctx // tm, ls // tm)
    ltri = jnp.asarray(np.tril(np.ones((tm, tm), np.float32), -1), dtype=BF16)
    n_assign = t_tot * TOP_K
    n_blocks = n_assign // bm + n_e
    rows = n_blocks * bm
    finals = []
    for l in range(depth):
        params = (small_b[l], a_row[l], ret_rep[l], wg[l], gb[l], ssd_conv_w[l], conv_b[l], d_row[l],
                  spread_b, spread_dt)
        init_lat = _pad_states(state_mlstm_C[:, l], state_mlstm_n[:, l], state_mlstm_m[:, l],
                               state_ret[:, l], state_gla[:, l], state_ssd[:, l])
        pm, pg = _proj(x, mod_all[l], norm1_g[l][None, :], w_main[l], w_gate[l], mod_map, tm)
        res_ctx = _mixers(pm, init_ctx, params, None, bp, lp, 0)
        res_lat = _mixers(pm, init_lat, params, rope, bs, ls, t_ctx)
        finals.append(_unpad_states(*res_ctx[2:]))
        x1, h2, e_idx, gates, rank, counts = _post(
            res_ctx[:2], res_lat[:2], pg, x, mod_all[l], hg[l], bd, w_out_b[l], norm2_g[l][None, :],
            rw[l], rb[l], ltri, n_e, mod_map, tm)

        cnt = counts[0, :n_e].astype(jnp.int32)
        padded = (cnt + bm - 1) // bm * bm
        pad_end = jnp.cumsum(padded)
        pad_start = (pad_end - padded).astype(jnp.int32)
        block_start = jnp.arange(n_blocks, dtype=jnp.int32) * bm
        block_e = jnp.minimum(jnp.sum((pad_end[None, :] <= block_start[:, None]).astype(jnp.int32), axis=1),
                              n_e - 1).astype(jnp.int32)
        n_used = (pad_end[-1] // bm).astype(jnp.int32).reshape(1)
        experts = jnp.arange(n_e, dtype=jnp.int32)
        seg_start = jnp.sum(jnp.where(e_idx[..., None] == experts, pad_start, 0), axis=-1)
        dest = (seg_start + rank).astype(jnp.int32).reshape(n_assign)

        xb = _dispatch(h2, dest, jnp.zeros((d // LANES, rows, LANES), F32), tm)
        yb = _moe(xb, block_e, n_used, moe_w_gu, moe_b_gu, moe_w_down, moe_b_down, l, bm)
        x = _combine(x1, yb, dest, gates, mod_all[l], final_g[None, :], l == depth - 1, mod_map, tm)

    y_prompt = x[:t_ctx].reshape(bp, lp, d)
    y_sample = x[t_ctx:].reshape(bs, ls, d)
    outs = [jnp.stack([f[i] for f in finals], axis=1) for i in range(6)]
    return (y_prompt, y_sample, *outs)
```

```python
import functools

import numpy as np
import jax
import jax.numpy as jnp
from jax import lax
from jax.experimental import pallas as pl
from jax.experimental.pallas import tpu as pltpu

F32 = jnp.float32
BF16 = jnp.bfloat16

EPS = 1e-6
CHUNK = 128
GRID_W = 64
ROPE_BASE = 10000.0
GLA_TAU = 16.0
TOP_K = 4
SWIGLU_LIMIT = 7.0
SWIGLU_ALPHA = 1.702
LANES = 128
SUBLANES = 8
VMEM_LIMIT = 56 * 1024 * 1024

MQ, MK, MV = 0, 256, 512
RQ, RK, RV = 768, 1024, 1280
GQ, GK, GV = 1536, 1664, 1792
SXBC = 2048
SMALL = 2560
P_MAIN = 2688
S_MI, S_MF, S_DT, S_DTRAW, S_GLR = 0, 8, 16, 24, 32


def _softplus(x):
    return jnp.maximum(x, 0.0) + jnp.log1p(jnp.exp(-jnp.abs(x)))


def _sigmoid(x):
    return 1.0 / (1.0 + jnp.exp(-x))


def _dot(a, b):
    return jnp.dot(a.astype(BF16), b.astype(BF16), preferred_element_type=F32)


def _dot_nt(a, b):
    return lax.dot_general(a.astype(BF16), b.astype(BF16), (((1,), (1,)), ((), ())),
                           preferred_element_type=F32)


def _split3(x):
    x1 = x.astype(BF16)
    r1 = x - x1.astype(F32)
    x2 = r1.astype(BF16)
    r2 = r1 - x2.astype(F32)
    return x1, x2, r2.astype(BF16)


def _mask_dot(mask_bf16, x):
    x1, x2, x3 = _split3(x)
    d = functools.partial(jnp.dot, preferred_element_type=F32)
    return (d(mask_bf16, x3) + d(mask_bf16, x2)) + d(mask_bf16, x1)


def _dot_mask(x, mask_bf16):
    x1, x2, x3 = _split3(x)
    d = functools.partial(jnp.dot, preferred_element_type=F32)
    return (d(x3, mask_bf16) + d(x2, mask_bf16)) + d(x1, mask_bf16)


def _dot_spread(x, sel_bf16):
    x1 = x.astype(BF16)
    x2 = (x - x1.astype(F32)).astype(BF16)
    d = functools.partial(jnp.dot, preferred_element_type=F32)
    return d(x2, sel_bf16) + d(x1, sel_bf16)


def _ada_kernel(c_ref, w_ref, b_ref, o_ref):
    c = c_ref[...]
    o_ref[0] = _dot(c * _sigmoid(c), w_ref[0]) + b_ref[0]


def _ada_mod(conds, ada_w, ada_b):
    depth, d, n6 = ada_w.shape
    rows = conds.shape[0]
    tn = 1536
    return pl.pallas_call(
        _ada_kernel,
        grid=(depth, n6 // tn),
        in_specs=[pl.BlockSpec((rows, d), lambda l, j: (0, 0)),
                  pl.BlockSpec((1, d, tn), lambda l, j: (l, 0, j)),
                  pl.BlockSpec((1, 1, tn), lambda l, j: (l, 0, j))],
        out_specs=pl.BlockSpec((1, rows, tn), lambda l, j: (l, 0, j)),
        out_shape=jax.ShapeDtypeStruct((depth, rows, n6), F32),
        compiler_params=pltpu.CompilerParams(dimension_semantics=("arbitrary", "arbitrary"),
                                             vmem_limit_bytes=VMEM_LIMIT),
        name="ada_mod",
    )(conds, ada_w, ada_b.reshape(depth, 1, n6))


def _proj_kernel(x_ref, mod_ref, g_ref, wm_ref, wg_ref, pm_ref, pg_ref):
    x = x_ref[...]
    y = x * lax.rsqrt(jnp.mean(x * x, axis=-1, keepdims=True) + EPS)
    h = (y * g_ref[...]) * (1.0 + mod_ref[0, 1:2, :]) + mod_ref[0, 0:1, :]
    hb = h.astype(BF16)
    pm_ref[...] = jnp.dot(hb, wm_ref[...], preferred_element_type=F32)
    pg_ref[...] = jnp.dot(hb, wg_ref[...], preferred_element_type=F32)


def _mod_map(n_ctx_tiles, tiles_per_latent_seq):
    def index(i, *_):
        return (jnp.where(i < n_ctx_tiles, 0, 1 + (i - n_ctx_tiles) // tiles_per_latent_seq), 0, 0)
    return index


def _proj(x, mod, g, w_main, w_gate, mod_map, tm):
    t, d = x.shape
    const = lambda i: (0, 0)
    return pl.pallas_call(
        _proj_kernel,
        grid=(t // tm,),
        in_specs=[pl.BlockSpec((tm, d), lambda i: (i, 0)),
                  pl.BlockSpec((1, 8, d), mod_map),
                  pl.BlockSpec((1, d), const),
                  pl.BlockSpec(w_main.shape, const),
                  pl.BlockSpec(w_gate.shape, const)],
        out_specs=[pl.BlockSpec((tm, P_MAIN), lambda i: (i, 0)),
                   pl.BlockSpec((tm, w_gate.shape[1]), lambda i: (i, 0))],
        out_shape=[jax.ShapeDtypeStruct((t, P_MAIN), F32),
                   jax.ShapeDtypeStruct((t, w_gate.shape[1]), F32)],
        compiler_params=pltpu.CompilerParams(dimension_semantics=("arbitrary",),
                                             vmem_limit_bytes=VMEM_LIMIT),
        name="in_proj",
    )(x, mod, g, w_main, w_gate)


def _mixer_direction(d, c_idx, n_chunks, use_rope, pm, hprev, hnext, cos_ref, sin_ref,
                     small_b, a_row, ret_rep, wg, gb, conv_w, conv_b, d_row, spread_b, spread_dt,
                     o_ref, mS, mm, rS, gS, sS, lat_s, cst_s):
    C = CHUNK
    last = C - 1 if d == 0 else 0
    row = lax.broadcasted_iota(jnp.int32, (C, C), 0)
    col = lax.broadcasted_iota(jnp.int32, (C, C), 1)
    mask = (col <= row) if d == 0 else (col >= row)
    mask_t = (row <= col) if d == 0 else (row >= col)
    tri = mask.astype(BF16)
    tri_t_ones = jnp.concatenate([mask_t.astype(BF16), jnp.ones((C, C), BF16)], axis=1)
    neg_inf = jnp.float32(-jnp.inf)
    lane = lax.broadcasted_iota(jnp.int32, (1, LANES), 1)
    even = lane < 64
    pair_sel = (even, jnp.logical_not(even))
    rows_of = (slice(0, 64), slice(64, 128))

    small = pm[:, SMALL:SMALL + LANES]
    pre = small + small_b[...]
    sp = _softplus(pre)
    lsg = -_softplus(-pre)
    a_neg = -jnp.exp(a_row[...])
    la = jnp.where(lane < S_MF, pre,
                   jnp.where(lane < S_DT, lsg,
                             jnp.where(lane < S_DTRAW, sp * a_neg, 0.0)))
    la_t = la.T
    lat_s[d] = la_t
    cst_s[d] = _dot_mask(la_t, tri_t_ones)
    cs = _mask_dot(tri, la)
    b_spread = _dot_spread(cs, spread_b[d])
    dt_spread = _dot_spread(sp, spread_dt[d])

    q_all = pm[:, MQ:MQ + 256] * (64.0 ** -0.5)
    k_all = pm[:, MK:MK + 256]
    v_all = pm[:, MV:MV + 256]
    k_t = k_all.T
    ones_blk = jnp.ones((C, LANES), BF16)
    for p in range(2):
        ps = slice(128 * p, 128 * p + 128)
        q_pair = q_all[:, ps].astype(BF16)
        k_pair = k_all[:, ps]
        vx = jnp.concatenate([v_all[:, ps].astype(BF16), ones_blk], axis=1)
        outs = []
        for r in range(2):
            h = 2 * p + r
            ci = d * 4 + h
            b_full = b_spread[:, 128 * h:128 * h + 128]
            b_row = cst_s[d, S_MF + ci:S_MF + ci + 1, 0:C]
            bl_row = cst_s[d, S_MF + ci:S_MF + ci + 1, C:2 * C]
            u_row = lat_s[d, S_MI + ci:S_MI + ci + 1, :] - b_row
            ms_row = mm[0, ci:ci + 1, :]
            d_log = jnp.where(mask, b_full + u_row, neg_inf)
            inter = b_full + ms_row
            m_t = jnp.maximum(inter, jnp.max(d_log, axis=1, keepdims=True))
            w = jnp.exp(d_log - m_t)
            k_m = jnp.where(pair_sel[r], k_pair, 0.0)
            sc = _dot_nt(q_pair, k_m) * w
            w_inter = jnp.exp(inter - m_t)
            s_old = mS[0, d, h]
            intra = _dot(sc, vx)
            cross = _dot(q_pair, s_old)
            num = intra[:, 0:LANES] + w_inter * cross[:, 0:LANES]
            den = intra[:, LANES:] + w_inter * cross[:, LANES:]
            outs.append(num / jnp.maximum(jnp.abs(den), jnp.exp(-m_t)))
            g_row = bl_row + u_row
            m_new = jnp.maximum(bl_row + ms_row, jnp.max(g_row, axis=1, keepdims=True))
            wk_row = jnp.exp(g_row - m_new)
            decay = jnp.exp(bl_row + ms_row - m_new)
            upd = _dot(k_t[64 * h:64 * h + 64, :] * wk_row, vx)
            live = rows_of[r]
            mS[0, d, h, live, 0:LANES] = decay * s_old[live, 0:LANES] + upd[:, 0:LANES]
            mS[0, d, h, live, LANES:] = decay * s_old[live, LANES:] + upd[:, LANES:]
            mm[0, ci:ci + 1, :] = m_new
        o_ref[:, ps] = jnp.where(even, outs[0], outs[1])

    rq = pm[:, RQ:RQ + 256]
    rk = pm[:, RK:RK + 256] * (64.0 ** -0.5)
    rv = pm[:, RV:RV + 256]
    if use_rope:
        cosv = cos_ref[...]
        sinv = sin_ref[...]
        first = (lax.broadcasted_iota(jnp.int32, (1, 256), 1) % 64) < 32

        def rope(x):
            rot = jnp.where(first, pltpu.roll(x, 256 - 32, 1), pltpu.roll(x, 32, 1))
            return x * cosv + rot * sinv

        rq = rope(rq)
        rk = rope(rk)
    rk_t = rk.T
    adiff = jnp.abs(row - col).astype(F32)
    ri_f = lax.broadcasted_iota(jnp.int32, (C, LANES), 0).astype(F32)
    cj_f = lax.broadcasted_iota(jnp.int32, (1, C), 1).astype(F32)
    ret_la = -jnp.exp(ret_rep[...])
    for p in range(2):
        ps = slice(128 * p, 128 * p + 128)
        q_pair = rq[:, ps].astype(BF16)
        k_pair = rk[:, ps]
        v_pair = rv[:, ps].astype(BF16)
        outs = []
        for r in range(2):
            h = 2 * p + r
            ci = d * 4 + h
            la_h = ret_la[ci:ci + 1, :]
            dec = jnp.where(mask, jnp.exp(la_h * adiff), 0.0)
            if d == 0:
                eb = jnp.exp(la_h * (ri_f + 1.0))
                kw_row = jnp.exp(la_h * (float(C - 1) - cj_f))
            else:
                eb = jnp.exp(la_h * (float(C) - ri_f))
                kw_row = jnp.exp(la_h * cj_f)
            k_m = jnp.where(pair_sel[r], k_pair, 0.0)
            sc = _dot_nt(q_pair, k_m) * dec
            s_old = rS[0, d, h]
            outs.append(_dot(sc, v_pair) + eb * _dot(q_pair, s_old))
            live = rows_of[r]
            rS[0, d, h, live, :] = (jnp.exp(la_h * float(C)) * s_old[live, :]
                                    + _dot(rk_t[64 * h:64 * h + 64, :] * kw_row, v_pair))
        o_ref[:, 256 + 128 * p:256 + 128 * p + 128] = jnp.where(even, outs[0], outs[1])

    gate = _dot(small, wg[d]) + gb[d]
    lga = (-_softplus(-gate)) * (1.0 / GLA_TAU)
    b = _mask_dot(tri, lga)
    bl_row = b[last:last + 1, :]
    b_mid = b[C // 2:C // 2 + 1, :]
    gq = pm[:, GQ:GQ + 128]
    gk = pm[:, GK:GK + 128] * (32.0 ** -0.5)
    gv = pm[:, GV:GV + 256]
    qc = (gq * jnp.exp(b - b_mid)).astype(BF16)
    kc = gk * jnp.exp(b_mid - b)
    qe = (gq * jnp.exp(b)).astype(BF16)
    kd = gk * jnp.exp(bl_row - b)
    ebl = jnp.exp(bl_row)
    gv_t = gv.T
    key_head = lane // 32
    for p in range(2):
        v_pair = gv[:, 128 * p:128 * p + 128].astype(BF16)
        outs = []
        for r in range(2):
            h = 2 * p + r
            mine = key_head == h
            sc = jnp.where(mask, _dot_nt(qc, jnp.where(mine, kc, 0.0)), 0.0)
            st_old = gS[0, d, h]
            outs.append(_dot(sc, v_pair) + _dot_nt(qe, st_old))
            live = rows_of[r]
            gS[0, d, h, live, :] = (st_old[live, :] * ebl
                                    + _dot(gv_t[64 * h:64 * h + 64, :], jnp.where(mine, kd, 0.0)))
        o_ref[:, 512 + 128 * p:512 + 128 * p + 128] = jnp.where(even, outs[0], outs[1])

    xr = pm[:, SXBC:SXBC + 512]
    prev = jnp.where(c_idx > 0, hprev[SUBLANES - 1:SUBLANES, :], 0.0)
    nxt = jnp.where(c_idx < n_chunks - 1, hnext[0:1, :], 0.0)
    rowi = lax.broadcasted_iota(jnp.int32, (C, 512), 0)
    x_dn = jnp.where(rowi == 0, prev, pltpu.roll(xr, 1, 0))
    x_up = jnp.where(rowi == C - 1, nxt, pltpu.roll(xr, C - 1, 0))
    y = conv_w[0:1, :] * x_dn + conv_w[1:2, :] * xr + conv_w[2:3, :] * x_up + conv_b[...]
    xbc = y * _sigmoid(y)
    sx = xbc[:, 0:256]
    s_b = xbc[:, 256:384]
    s_c = xbc[:, 384:512].astype(BF16)
    sb_t = s_b.T
    xdt = sx * dt_spread
    for p in range(2):
        ps = slice(128 * p, 128 * p + 128)
        v_pair = xdt[:, ps].astype(BF16)
        raw = _dot_nt(s_c, jnp.where(pair_sel[p], s_b, 0.0))
        outs = []
        for r in range(2):
            h = 2 * p + r
            ci = d * 4 + h
            b_full = b_spread[:, 512 + 128 * h:512 + 128 * h + 128]
            b_row = cst_s[d, S_DT + ci:S_DT + ci + 1, 0:C]
            bl_row = cst_s[d, S_DT + ci:S_DT + ci + 1, C:2 * C]
            dec = jnp.exp(jnp.where(mask, b_full - b_row, neg_inf))
            s_old = sS[0, d, h]
            outs.append(_dot(raw * dec, v_pair) + jnp.exp(b_full) * _dot(s_c, s_old))
            live = rows_of[p]
            sS[0, d, h, live, :] = (jnp.exp(bl_row) * s_old[live, :]
                                    + _dot(sb_t[64 * p:64 * p + 64, :] * jnp.exp(bl_row - b_row), v_pair))
        o = jnp.where(even, outs[0], outs[1])
        if d == 0:
            o = o + d_row[:, ps] * sx[:, ps]
        o_ref[:, 768 + 128 * p:768 + 128 * p + 128] = o


def _mixer_kernel(n_chunks, use_rope, *refs):
    it = iter(refs)
    pm_f, pm_b, hpf, hnf, hpb, hnb = (next(it) for _ in range(6))
    if use_rope:
        cos_f, sin_f, cos_b, sin_b = (next(it) for _ in range(4))
    else:
        cos_f = sin_f = cos_b = sin_b = None
    mS0, mm0, r0, g0, s0 = (next(it) for _ in range(5))
    params = tuple(next(it) for _ in range(10))
    o_f, o_b, mS, mm, rS, gS, sS = (next(it) for _ in range(7))
    lat_s, cst_s = (next(it) for _ in range(2))

    s = pl.program_id(1)

    @pl.when(s == 0)
    def _():
        mS[...] = mS0[...]
        mm[...] = mm0[...]
        rS[...] = r0[...]
        gS[...] = g0[...]
        sS[...] = s0[...]

    _mixer_direction(0, s, n_chunks, use_rope, pm_f, hpf, hnf, cos_f, sin_f, *params,
                     o_f, mS, mm, rS, gS, sS, lat_s, cst_s)
    _mixer_direction(1, n_chunks - 1 - s, n_chunks, use_rope, pm_b, hpb, hnb, cos_b, sin_b, *params,
                     o_b, mS, mm, rS, gS, sS, lat_s, cst_s)


def _mixers(pm, init, params, rope, n_seq, seq_len, row0):
    C = CHUNK
    n = seq_len // C
    t = pm.shape[0]
    chunk0 = row0 // C
    rb = C // SUBLANES
    n_rb = t // SUBLANES
    xbc_blk = SXBC // 512
    use_rope = rope is not None

    def f_idx(b, s):
        return chunk0 + b * n + s

    def b_idx(b, s):
        return chunk0 + b * n + (n - 1 - s)

    in_specs = [
        pl.BlockSpec((C, P_MAIN), lambda b, s: (f_idx(b, s), 0)),
        pl.BlockSpec((C, P_MAIN), lambda b, s: (b_idx(b, s), 0)),
        pl.BlockSpec((SUBLANES, 512), lambda b, s: (jnp.maximum(f_idx(b, s) * rb - 1, 0), xbc_blk)),
        pl.BlockSpec((SUBLANES, 512), lambda b, s: (jnp.minimum(f_idx(b, s) * rb + rb, n_rb - 1), xbc_blk)),
        pl.BlockSpec((SUBLANES, 512), lambda b, s: (jnp.maximum(b_idx(b, s) * rb - 1, 0), xbc_blk)),
        pl.BlockSpec((SUBLANES, 512), lambda b, s: (jnp.minimum(b_idx(b, s) * rb + rb, n_rb - 1), xbc_blk)),
    ]
    args = [pm, pm, pm, pm, pm, pm]
    if use_rope:
        cos, sin = rope
        in_specs += [pl.BlockSpec((C, 256), lambda b, s: (s, 0)),
                     pl.BlockSpec((C, 256), lambda b, s: (s, 0)),
                     pl.BlockSpec((C, 256), lambda b, s: (n - 1 - s, 0)),
                     pl.BlockSpec((C, 256), lambda b, s: (n - 1 - s, 0))]
        args += [cos, sin, cos, sin]

    def state_spec(shape):
        blk = (1,) + tuple(shape[1:])
        nd = len(shape)
        return pl.BlockSpec(blk, lambda b, s: (b,) + (0,) * (nd - 1))

    for a in init:
        in_specs.append(state_spec(a.shape))
        args.append(a)
    for p in params:
        nd = p.ndim
        in_specs.append(pl.BlockSpec(p.shape, lambda b, s, nd=nd: (0,) * nd))
        args.append(p)

    t_out = n_seq * seq_len
    out_specs = [pl.BlockSpec((C, 1024), lambda b, s: (b * n + s, 0)),
                 pl.BlockSpec((C, 1024), lambda b, s: (b * n + (n - 1 - s), 0))]
    out_shape = [jax.ShapeDtypeStruct((t_out, 1024), F32), jax.ShapeDtypeStruct((t_out, 1024), F32)]
    for a in init:
        out_specs.append(state_spec(a.shape))
        out_shape.append(jax.ShapeDtypeStruct(a.shape, F32))

    scratch = [pltpu.VMEM((2, LANES, C), F32), pltpu.VMEM((2, LANES, 2 * C), F32)]
    return pl.pallas_call(
        functools.partial(_mixer_kernel, n, use_rope),
        grid=(n_seq, n),
        in_specs=in_specs,
        out_specs=out_specs,
        out_shape=out_shape,
        scratch_shapes=scratch,
        compiler_params=pltpu.CompilerParams(dimension_semantics=("arbitrary", "arbitrary"),
                                             vmem_limit_bytes=VMEM_LIMIT),
        name="mixers_rope" if use_rope else "mixers",
    )(*args)


def _post_kernel(n_experts, n_ctx_tiles, ofc_ref, obc_ref, ofl_ref, obl_ref, pg_ref, x_ref, mod_ref,
                 hg_ref, bd_ref, wo_ref, n2_ref, rw_ref, rb_ref, ltri_ref,
                 x1_ref, h2_ref, ei_ref, gt_ref, rk_ref, cnt_ref):
    @pl.when(pl.program_id(0) == 0)
    def _():
        cnt_ref[...] = jnp.zeros_like(cnt_ref)

    y = jnp.where(pl.program_id(0) < n_ctx_tiles, ofc_ref[...] + obc_ref[...], ofl_ref[...] + obl_ref[...])
    gt = pg_ref[...]
    lane = lax.broadcasted_iota(jnp.int32, (1, y.shape[1]), 1)
    sig = _sigmoid(gt)
    silu = gt * sig
    pre = y * jnp.where(lane >= 768, silu, 1.0)
    sq = (pre * pre).astype(BF16)
    ms = jnp.concatenate([jnp.dot(sq[:, LANES * s:LANES * s + LANES], bd_ref[...], preferred_element_type=F32)
                          for s in range(y.shape[1] // LANES)], axis=1) * (1.0 / 64.0)
    nrm = pre * lax.rsqrt(ms + EPS) * hg_ref[...]
    post = jnp.where(lane < 256, sig, jnp.where(lane < 768, silu, 1.0))
    mixed = jnp.dot((nrm * post).astype(BF16), wo_ref[...], preferred_element_type=F32)
    x1 = x_ref[...] + mod_ref[0, 2:3, :] * mixed
    x1_ref[...] = x1
    yn = x1 * lax.rsqrt(jnp.mean(x1 * x1, axis=-1, keepdims=True) + EPS)
    h2 = (yn * n2_ref[...]) * (1.0 + mod_ref[0, 4:5, :]) + mod_ref[0, 3:4, :]
    bits = lax.bitcast_convert_type(h2.astype(BF16).astype(F32), jnp.uint32)
    half = h2.shape[1] // 2
    for s in range(h2_ref.shape[0]):
        lo = bits[:, LANES * s:LANES * s + LANES]
        hi = bits[:, half + LANES * s:half + LANES * s + LANES]
        h2_ref[s] = lo | (hi >> 16)

    tm = h2.shape[0]
    lane_f = lax.broadcasted_iota(jnp.int32, (1, LANES), 1).astype(F32)
    neg_inf = jnp.float32(-jnp.inf)
    logits = _dot(h2, rw_ref[...]) + rb_ref[...]
    work = jnp.where(lane_f < float(n_experts), logits, neg_inf)
    vals, idxs, hots = [], [], []
    for _ in range(TOP_K):
        m = jnp.max(work, axis=1, keepdims=True)
        idx = jnp.min(jnp.where(work == m, lane_f, float(LANES)), axis=1, keepdims=True)
        hot = lane_f == idx
        vals.append(m)
        idxs.append(idx)
        hots.append(hot)
        work = jnp.where(hot, neg_inf, work)
    exps = [jnp.exp(v - vals[0]) for v in vals]
    denom = exps[0]
    for e in exps[1:]:
        denom = denom + e
    sel = hots[0].astype(F32)
    for hot in hots[1:]:
        sel = sel + hot.astype(F32)
    before = jnp.dot(ltri_ref[...], sel.astype(BF16), preferred_element_type=F32) + cnt_ref[...]
    ranks = [jnp.sum(jnp.where(hot, before, 0.0), axis=1, keepdims=True) for hot in hots]
    cnt_ref[...] = cnt_ref[...] + jnp.sum(sel, axis=0, keepdims=True)
    slot = lax.broadcasted_iota(jnp.int32, (tm, TOP_K), 1)

    def pack(cols):
        out = cols[TOP_K - 1]
        for k in range(TOP_K - 2, -1, -1):
            out = jnp.where(slot == k, cols[k], out)
        return out

    ei_ref[...] = pack(idxs).astype(jnp.int32)
    gt_ref[...] = pack([e / denom for e in exps])
    rk_ref[...] = pack(ranks).astype(jnp.int32)


def _post(o_ctx, o_lat, pg, x, mod, hg, bd, w_out, n2g, rw, rb, ltri, n_experts, mod_map, tm):
    t, d = x.shape
    n_slab = d // LANES
    nct = o_ctx[0].shape[0] // tm
    row = lambda i: (i, 0)
    const = lambda i: (0, 0)
    ctx_row = lambda i: (jnp.minimum(i, nct - 1), 0)
    lat_row = lambda i: (jnp.maximum(i - nct, 0), 0)
    return pl.pallas_call(
        functools.partial(_post_kernel, n_experts, nct),
        grid=(t // tm,),
        in_specs=[pl.BlockSpec((tm, d), ctx_row), pl.BlockSpec((tm, d), ctx_row),
                  pl.BlockSpec((tm, d), lat_row), pl.BlockSpec((tm, d), lat_row),
                  pl.BlockSpec((tm, d), row),
                  pl.BlockSpec((tm, d), row), pl.BlockSpec((1, 8, d), mod_map),
                  pl.BlockSpec((1, d), const), pl.BlockSpec((LANES, LANES), const), pl.BlockSpec((d, d), const),
                  pl.BlockSpec((1, d), const), pl.BlockSpec((d, LANES), const), pl.BlockSpec((1, LANES), const),
                  pl.BlockSpec((tm, tm), const)],
        out_specs=[pl.BlockSpec((tm, d), row), pl.BlockSpec((n_slab // 2, tm, LANES), lambda i: (0, i, 0)),
                   pl.BlockSpec((tm, TOP_K), row), pl.BlockSpec((tm, TOP_K), row),
                   pl.BlockSpec((tm, TOP_K), row), pl.BlockSpec((1, LANES), const)],
        out_shape=[jax.ShapeDtypeStruct((t, d), F32), jax.ShapeDtypeStruct((n_slab // 2, t, LANES), jnp.uint32),
                   jax.ShapeDtypeStruct((t, TOP_K), jnp.int32), jax.ShapeDtypeStruct((t, TOP_K), F32),
                   jax.ShapeDtypeStruct((t, TOP_K), jnp.int32), jax.ShapeDtypeStruct((1, LANES), F32)],
        compiler_params=pltpu.CompilerParams(dimension_semantics=("arbitrary",),
                                             vmem_limit_bytes=VMEM_LIMIT),
        name="post_mix",
    )(o_ctx[0], o_ctx[1], o_lat[0], o_lat[1], pg, x, mod, hg, bd, w_out, n2g, rw, rb, ltri)


def _dispatch_kernel(tm, dest_ref, h_ref, xb_in, xb_out, sem):
    del xb_in

    def row_copy(t, k):
        return pltpu.make_async_copy(h_ref.at[:, pl.ds(t, 1), :],
                                     xb_out.at[:, pl.ds(dest_ref[t * TOP_K + k], 1), :], sem)

    def start(t, c):
        for k in range(TOP_K):
            row_copy(t, k).start(priority=k % 2)
        return c

    def wait(t, c):
        for k in range(TOP_K):
            row_copy(t, k).wait()
        return c

    lax.fori_loop(0, tm, start, 0, unroll=8)
    lax.fori_loop(0, tm, wait, 0, unroll=8)


def _dispatch(h2, dest, xb_zero, tm):
    n_slab, t, _ = h2.shape
    return pl.pallas_call(
        functools.partial(_dispatch_kernel, tm),
        grid=(t // tm,),
        in_specs=[pl.BlockSpec((tm * TOP_K,), lambda i: (i,), memory_space=pltpu.SMEM),
                  pl.BlockSpec((n_slab, tm, LANES), lambda i: (0, i, 0)),
                  pl.BlockSpec(memory_space=pl.ANY)],
        out_specs=pl.BlockSpec(memory_space=pl.ANY),
        out_shape=jax.ShapeDtypeStruct(xb_zero.shape, xb_zero.dtype),
        scratch_shapes=[pltpu.SemaphoreType.DMA(())],
        input_output_aliases={2: 0},
        compiler_params=pltpu.CompilerParams(dimension_semantics=("arbitrary",),
                                             vmem_limit_bytes=VMEM_LIMIT),
        name="moe_dispatch",
    )(dest, h2, xb_zero)


def _moe_kernel(be_ref, nu_ref, xb_ref, wgu_ref, bgu_ref, wdn_ref, bdn_ref, o_ref, wgu_b, wdn_b):
    i = pl.program_id(0)
    d_ff = wdn_ref.shape[2]
    n_slab = xb_ref.shape[0]
    active = i < nu_ref[0]
    new_expert = jnp.logical_or(i == 0, be_ref[i] != be_ref[jnp.maximum(i - 1, 0)])

    @pl.when(jnp.logical_and(active, new_expert))
    def _():
        wgu_b[...] = wgu_ref[0, 0].astype(BF16)
        wdn_b[...] = wdn_ref[0, 0].astype(BF16)

    @pl.when(active)
    def _():
        words = [xb_ref[s] for s in range(n_slab)]
        lo = [lax.bitcast_convert_type(w & jnp.uint32(0xFFFF0000), F32).astype(BF16) for w in words]
        hi = [lax.bitcast_convert_type(w << 16, F32).astype(BF16) for w in words]
        x = jnp.concatenate(lo + hi, axis=1)
        gu = jnp.dot(x, wgu_b[...], preferred_element_type=F32) + bgu_ref[0, 0]
        gate = jnp.minimum(gu[:, :d_ff], SWIGLU_LIMIT)
        up = jnp.clip(gu[:, d_ff:], -SWIGLU_LIMIT, SWIGLU_LIMIT)
        act = gate * _sigmoid(SWIGLU_ALPHA * gate) * (up + 1.0)
        out = jnp.dot(act.astype(BF16), wdn_b[...], preferred_element_type=F32) + bdn_ref[0, 0]
        for s in range(o_ref.shape[0]):
            o_ref[s] = out[:, LANES * s:LANES * s + LANES]

    @pl.when(jnp.logical_not(active))
    def _():
        o_ref[...] = jnp.zeros_like(o_ref)


def _moe(xb, block_e, n_used, w_gu, b_gu, w_dn, b_dn, layer, bm):
    n_words, rows, _ = xb.shape
    depth, n_e, d, n_gu = w_gu.shape
    d_ff = w_dn.shape[2]
    n_slab = d // LANES
    grid_spec = pltpu.PrefetchScalarGridSpec(
        num_scalar_prefetch=2,
        grid=(rows // bm,),
        in_specs=[pl.BlockSpec((n_words, bm, LANES), lambda i, be, nu: (0, jnp.minimum(i, nu[0] - 1), 0)),
                  pl.BlockSpec((1, 1, d, n_gu), lambda i, be, nu: (layer, be[i], 0, 0)),
                  pl.BlockSpec((1, 1, 1, n_gu), lambda i, be, nu: (layer, be[i], 0, 0)),
                  pl.BlockSpec((1, 1, d_ff, d), lambda i, be, nu: (layer, be[i], 0, 0)),
                  pl.BlockSpec((1, 1, 1, d), lambda i, be, nu: (layer, be[i], 0, 0))],
        out_specs=pl.BlockSpec((n_slab, bm, LANES), lambda i, be, nu: (0, i, 0)),
        scratch_shapes=[pltpu.VMEM((d, n_gu), BF16), pltpu.VMEM((d_ff, d), BF16)],
    )
    return pl.pallas_call(
        _moe_kernel,
        grid_spec=grid_spec,
        out_shape=jax.ShapeDtypeStruct((n_slab, rows, LANES), F32),
        compiler_params=pltpu.CompilerParams(dimension_semantics=("arbitrary",),
                                             vmem_limit_bytes=VMEM_LIMIT),
        name="moe_ffn",
    )(block_e, n_used, xb, w_gu, b_gu.reshape(depth, n_e, 1, n_gu), w_dn, b_dn.reshape(depth, n_e, 1, d))


def _combine_kernel(final, tm, dest_ref, x1_ref, gt_ref, mod_ref, fg_ref, yb_ref, o_ref, ybuf, sem):
    def row_copy(t, k):
        return pltpu.make_async_copy(yb_ref.at[:, pl.ds(dest_ref[t * TOP_K + k], 1), :],
                                     ybuf.at[k, :, pl.ds(t, 1), :], sem)

    def start(t, c):
        for k in range(TOP_K):
            row_copy(t, k).start(priority=k % 2)
        return c

    def wait(t, c):
        for k in range(TOP_K):
            row_copy(t, k).wait()
        return c

    lax.fori_loop(0, tm, start, 0, unroll=8)
    lax.fori_loop(0, tm, wait, 0, unroll=8)

    n_slab = ybuf.shape[1]
    gt = gt_ref[...]
    g_full = [jnp.broadcast_to(gt[:, k:k + 1], (tm, LANES)) for k in range(TOP_K)]
    slabs = []
    for s in range(n_slab):
        cs = slice(LANES * s, LANES * s + LANES)
        y = g_full[0] * ybuf[0, s]
        for k in range(1, TOP_K):
            y = y + g_full[k] * ybuf[k, s]
        slabs.append(x1_ref[:, cs] + mod_ref[0, 5:6, cs] * y)
    x2 = jnp.concatenate(slabs, axis=1)
    if final:
        x2 = x2 * lax.rsqrt(jnp.mean(x2 * x2, axis=-1, keepdims=True) + EPS) * fg_ref[...]
    o_ref[...] = x2


def _combine(x1, yb, dest, gates, mod, final_g, final, mod_map, tm):
    t, d = x1.shape
    n_slab = d // LANES
    return pl.pallas_call(
        functools.partial(_combine_kernel, final, tm),
        grid=(t // tm,),
        in_specs=[pl.BlockSpec((tm * TOP_K,), lambda i: (i,), memory_space=pltpu.SMEM),
                  pl.BlockSpec((tm, d), lambda i: (i, 0)),
                  pl.BlockSpec((tm, TOP_K), lambda i: (i, 0)),
                  pl.BlockSpec((1, 8, d), mod_map),
                  pl.BlockSpec((1, d), lambda i: (0, 0)),
                  pl.BlockSpec(memory_space=pl.ANY)],
        out_specs=pl.BlockSpec((tm, d), lambda i: (i, 0)),
        out_shape=jax.ShapeDtypeStruct((t, d), F32),
        scratch_shapes=[pltpu.VMEM((TOP_K, n_slab, tm, LANES), F32), pltpu.SemaphoreType.DMA(())],
        compiler_params=pltpu.CompilerParams(dimension_semantics=("arbitrary",),
                                             vmem_limit_bytes=VMEM_LIMIT),
        name="moe_combine",
    )(dest, x1, gates, mod, final_g, yb)


def _rope_tables(n_tokens):
    rows = n_tokens // GRID_W
    r, c = jnp.meshgrid(jnp.arange(rows), jnp.arange(GRID_W), indexing='ij')
    nf = 16
    inv = ROPE_BASE ** (-jnp.arange(nf, dtype=F32) / nf)
    ang = jnp.concatenate([r.reshape(-1, 1).astype(F32) * inv, c.reshape(-1, 1).astype(F32) * inv], axis=-1)
    cos, sin = jnp.cos(ang), jnp.sin(ang)
    cos_t = jnp.tile(jnp.concatenate([cos, cos], axis=-1), (1, 4))
    sin_t = jnp.tile(jnp.concatenate([-sin, sin], axis=-1), (1, 4))
    return cos_t, sin_t


def _pad_lanes(a, width=LANES):
    return jnp.pad(a, [(0, 0)] * (a.ndim - 1) + [(0, width - a.shape[-1])])


def _spread_matrices():
    sb = np.zeros((2, LANES, 1024), np.float32)
    sdt = np.zeros((2, LANES, 256), np.float32)
    for d in range(2):
        for h in range(4):
            ci = d * 4 + h
            sb[d, S_MF + ci, 128 * h:128 * h + 128] = 1.0
            sb[d, S_DT + ci, 512 + 128 * h:512 + 128 * h + 128] = 1.0
            sdt[d, S_DT + ci, 64 * h:64 * h + 64] = 1.0
    return jnp.asarray(sb, dtype=BF16), jnp.asarray(sdt, dtype=BF16)


def _pad_states(c_m, n_m, m_m, s_ret, s_gla, s_ssd):
    bsz = c_m.shape[0]

    def place(a, row0, n_rows, col0, n_cols):
        r, c = a.shape[-2:]
        return jnp.pad(a, ((0, 0), (0, 0), (row0, n_rows - row0 - r), (col0, n_cols - col0 - c)))

    mS, rS, gS, sS = [], [], [], []
    for h in range(4):
        r, g = h % 2, h // 2
        n_rep = jnp.broadcast_to(n_m[:, :, h, :, None], (bsz, 2, 64, 128))
        mS.append(jnp.concatenate([place(c_m[:, :, h], 64 * r, 128, 64 * r, 128),
                                   place(n_rep, 64 * r, 128, 0, 128)], axis=-1))
        rS.append(place(s_ret[:, :, h], 64 * r, 128, 64 * r, 128))
        gS.append(place(jnp.swapaxes(s_gla[:, :, h], -1, -2), 64 * r, 128, 32 * h, 128))
        sS.append(place(s_ssd[:, :, h], 64 * g, 128, 64 * r, 128))
    stack = lambda xs: jnp.stack(xs, axis=2)
    mm = jnp.broadcast_to(m_m.reshape(bsz, 8, 1), (bsz, 8, LANES))
    return stack(mS), mm, stack(rS), stack(gS), stack(sS)


def _unpad_states(mS, mm, rS, gS, sS):
    bsz = mS.shape[0]
    c_m, n_m, s_ret, s_gla, s_ssd = [], [], [], [], []
    for h in range(4):
        r, g = h % 2, h // 2
        lv = slice(64 * r, 64 * r + 64)
        c_m.append(mS[:, :, h, lv, lv])
        n_m.append(mS[:, :, h, lv, 128])
        s_ret.append(rS[:, :, h, lv, lv])
        s_gla.append(jnp.swapaxes(gS[:, :, h, lv, 32 * h:32 * h + 32], -1, -2))
        s_ssd.append(sS[:, :, h, 64 * g:64 * g + 64, lv])
    stack = lambda xs: jnp.stack(xs, axis=2)
    return (stack(c_m), stack(n_m), mm[:, :, 0].reshape(bsz, 2, 4), stack(s_ret), stack(s_gla), stack(s_ssd))


def kernel(x_prompt, x_sample, c, state_mlstm_C, state_mlstm_n, state_mlstm_m, state_ret, state_gla, state_ssd,
           c_ctx, ada_w, ada_b, norm1_g, norm2_g, final_g, w_in, w_out,
           mlstm_i_b, mlstm_f_b, mlstm_norm_g, ret_decay, ret_norm_g,
           gla_gate_w, gla_gate_b, gla_norm_g,
           ssd_conv_w, ssd_conv_b, ssd_dt_b, ssd_A_log, ssd_D, ssd_norm_g,
           router_w, router_b, moe_w_gu, moe_b_gu, moe_w_down, moe_b_down):
    depth = w_in.shape[0]
    bp, lp, d = x_prompt.shape
    bs, ls, _ = x_sample.shape
    n_e = router_w.shape[-1]
    tm = 256
    bm = 512

    offs = np.cumsum([0, 256, 256, 256, 256, 8, 8, 256, 256, 256, 256, 128, 128, 256, 256, 32, 256, 512, 8])
    (o_mq, o_mk, o_mv, o_mo, o_mi, o_mf, o_rq, o_rk, o_rv, o_rg, o_gq, o_gk, o_gv, o_gr, o_glr,
     o_sz, o_sxbc, o_sdt, _) = [int(v) for v in offs]

    def cols(a, n):
        return w_in[:, :, a:a + n]

    small_w = jnp.concatenate([cols(o_mi, 8), cols(o_mf, 8), cols(o_sdt, 8),
                               jnp.zeros((depth, d, 8), F32), cols(o_glr, 32),
                               jnp.zeros((depth, d, LANES - 64), F32)], axis=-1)
    w_main = jnp.concatenate([cols(o_mq, 256), cols(o_mk, 256), cols(o_mv, 256),
                              cols(o_rq, 256), cols(o_rk, 256), cols(o_rv, 256),
                              cols(o_gq, 128), cols(o_gk, 128), cols(o_gv, 256),
                              cols(o_sxbc, 512), small_w], axis=-1).astype(BF16)
    w_gate = jnp.concatenate([cols(o_mo, 256), cols(o_rg, 256), cols(o_gr, 256), cols(o_sz, 256)],
                             axis=-1).astype(BF16)
    w_out_b = w_out.astype(BF16)
    rw = _pad_lanes(router_w).astype(BF16)
    rb = _pad_lanes(router_b)[:, None, :]
    hg = jnp.concatenate([mlstm_norm_g, ret_norm_g, gla_norm_g, ssd_norm_g], axis=-1)[:, None, :]
    gidx = np.arange(LANES) // 64
    bd = jnp.asarray(gidx[:, None] == gidx[None, :], dtype=BF16)

    small_b = _pad_lanes(jnp.concatenate([mlstm_i_b.reshape(depth, 8), mlstm_f_b.reshape(depth, 8),
                                          ssd_dt_b.reshape(depth, 8)], axis=-1))[:, None, :]
    a_row = _pad_lanes(jnp.concatenate([jnp.zeros((depth, 16), F32), ssd_A_log.reshape(depth, 8)],
                                       axis=-1))[:, None, :]
    ret_rep = jnp.broadcast_to(ret_decay.reshape(depth, 8, 1), (depth, 8, LANES))
    wg = jnp.zeros((depth, 2, LANES, LANES), F32)
    wg = wg.at[:, 0, S_GLR:S_GLR + 16, :].set(gla_gate_w[:, 0])
    wg = wg.at[:, 1, S_GLR + 16:S_GLR + 32, :].set(gla_gate_w[:, 1])
    wg = wg.astype(BF16)
    gb = gla_gate_b[:, :, None, :]
    d_row = jnp.repeat(ssd_D, 64, axis=-1)[:, None, :]
    conv_b = ssd_conv_b[:, None, :]
    spread_b, spread_dt = _spread_matrices()

    n_cond = 1 + bs
    pad_rows = (-n_cond) % 8
    conds = jnp.concatenate([c_ctx[None, :], c, jnp.zeros((pad_rows, d), F32)], axis=0)
    mod_all = _ada_mod(conds, ada_w, ada_b).reshape(depth, n_cond + pad_rows, 6, d)
    mod_all = jnp.pad(mod_all, ((0, 0), (0, 0), (0, 2), (0, 0)))

    rope = _rope_tables(ls)
    zero = lambda *s: jnp.zeros(s, F32)
    init_ctx = (zero(bp, 2, 4, 128, 256), zero(bp, 8, LANES), zero(bp, 2, 4, 128, 128),
                zero(bp, 2, 4, 128, 128), zero(bp, 2, 4, 128, 128))

    t_ctx = bp * lp
    t_tot = t_ctx + bs * ls
    x = jnp.concatenate([x_prompt.reshape(t_ctx, d), x_sample.reshape(bs * ls, d)], axis=0)
    mod_map = _mod_map(t_ctx // tm, ls // tm)
    ltri = jnp.asarray(np.tril(np.ones((tm, tm), np.float32), -1), dtype=BF16)
    n_assign = t_tot * TOP_K
    n_blocks = n_assign // bm + n_e
    rows = n_blocks * bm
    finals = []
    for l in range(depth):
        params = (small_b[l], a_row[l], ret_rep[l], wg[l], gb[l], ssd_conv_w[l], conv_b[l], d_row[l],
                  spread_b, spread_dt)
        init_lat = _pad_states(state_mlstm_C[:, l], state_mlstm_n[:, l], state_mlstm_m[:, l],
                               state_ret[:, l], state_gla[:, l], state_ssd[:, l])
        pm, pg = _proj(x, mod_all[l], norm1_g[l][None, :], w_main[l], w_gate[l], mod_map, tm)
        res_ctx = _mixers(pm, init_ctx, params, None, bp, lp, 0)
        res_lat = _mixers(pm, init_lat, params, rope, bs, ls, t_ctx)
        finals.append(_unpad_states(*res_ctx[2:]))
        x1, h2, e_idx, gates, rank, counts = _post(
            res_ctx[:2], res_lat[:2], pg, x, mod_all[l], hg[l], bd, w_out_b[l], norm2_g[l][None, :],
            rw[l], rb[l], ltri, n_e, mod_map, tm)

        cnt = counts[0, :n_e].astype(jnp.int32)
        padded = (cnt + bm - 1) // bm * bm
        pad_end = jnp.cumsum(padded)
        pad_start = (pad_end - padded).astype(jnp.int32)
        block_start = jnp.arange(n_blocks, dtype=jnp.int32) * bm
        block_e = jnp.minimum(jnp.sum((pad_end[None, :] <= block_start[:, None]).astype(jnp.int32), axis=1),
                              n_e - 1).astype(jnp.int32)
        n_used = (pad_end[-1] // bm).astype(jnp.int32).reshape(1)
        experts = jnp.arange(n_e, dtype=jnp.int32)
        seg_start = jnp.sum(jnp.where(e_idx[..., None] == experts, pad_start, 0), axis=-1)
        dest = (seg_start + rank).astype(jnp.int32).reshape(n_assign)

        xb = _dispatch(h2, dest, jnp.zeros((h2.shape[0], rows, LANES), h2.dtype), tm)
        yb = _moe(xb, block_e, n_used, moe_w_gu, moe_b_gu, moe_w_down, moe_b_down, l, bm)
        x = _combine(x1, yb, dest, gates, mod_all[l], final_g[None, :], l == depth - 1, mod_map, tm)

    y_prompt = x[:t_ctx].reshape(bp, lp, d)
    y_sample = x[t_ctx:].reshape(bs, ls, d)
    outs = [jnp.stack([f[i] for f in finals], axis=1) for i in range(6)]
    return (y_prompt, y_sample, *outs)
```

```python
import functools

import numpy as np
import jax
import jax.numpy as jnp
from jax import lax
from jax.experimental import pallas as pl
from jax.experimental.pallas import tpu as pltpu

F32 = jnp.float32
BF16 = jnp.bfloat16

EPS = 1e-6
CHUNK = 128
GRID_W = 64
ROPE_BASE = 10000.0
GLA_TAU = 16.0
TOP_K = 4
SWIGLU_LIMIT = 7.0
SWIGLU_ALPHA = 1.702
LANES = 128
SUBLANES = 8
VMEM_LIMIT = 56 * 1024 * 1024

MQ, MK, MV = 0, 256, 512
RQ, RK, RV = 768, 1024, 1280
GQ, GK, GV = 1536, 1664, 1792
SXBC = 2048
SMALL = 2560
P_MAIN = 2688
S_MI, S_MF, S_DT, S_DTRAW, S_GLR = 0, 8, 16, 24, 32


def _softplus(x):
    return jnp.maximum(x, 0.0) + jnp.log1p(jnp.exp(-jnp.abs(x)))


def _sigmoid(x):
    return 1.0 / (1.0 + jnp.exp(-x))


def _dot(a, b):
    return jnp.dot(a.astype(BF16), b.astype(BF16), preferred_element_type=F32)


def _dot_nt(a, b):
    return lax.dot_general(a.astype(BF16), b.astype(BF16), (((1,), (1,)), ((), ())),
                           preferred_element_type=F32)


def _split3(x):
    x1 = x.astype(BF16)
    r1 = x - x1.astype(F32)
    x2 = r1.astype(BF16)
    r2 = r1 - x2.astype(F32)
    return x1, x2, r2.astype(BF16)


def _mask_dot(mask_bf16, x):
    x1, x2, x3 = _split3(x)
    d = functools.partial(jnp.dot, preferred_element_type=F32)
    return (d(mask_bf16, x3) + d(mask_bf16, x2)) + d(mask_bf16, x1)


def _dot_mask(x, mask_bf16):
    x1, x2, x3 = _split3(x)
    d = functools.partial(jnp.dot, preferred_element_type=F32)
    return (d(x3, mask_bf16) + d(x2, mask_bf16)) + d(x1, mask_bf16)


def _dot_spread(x, sel_bf16):
    x1 = x.astype(BF16)
    x2 = (x - x1.astype(F32)).astype(BF16)
    d = functools.partial(jnp.dot, preferred_element_type=F32)
    return d(x2, sel_bf16) + d(x1, sel_bf16)


def _ada_kernel(c_ref, w_ref, b_ref, o_ref):
    c = c_ref[...]
    o_ref[0] = _dot(c * _sigmoid(c), w_ref[0]) + b_ref[0]


def _ada_mod(conds, ada_w, ada_b):
    depth, d, n6 = ada_w.shape
    rows = conds.shape[0]
    tn = 1536
    return pl.pallas_call(
        _ada_kernel,
        grid=(depth, n6 // tn),
        in_specs=[pl.BlockSpec((rows, d), lambda l, j: (0, 0)),
                  pl.BlockSpec((1, d, tn), lambda l, j: (l, 0, j)),
                  pl.BlockSpec((1, 1, tn), lambda l, j: (l, 0, j))],
        out_specs=pl.BlockSpec((1, rows, tn), lambda l, j: (l, 0, j)),
        out_shape=jax.ShapeDtypeStruct((depth, rows, n6), F32),
        compiler_params=pltpu.CompilerParams(dimension_semantics=("arbitrary", "arbitrary"),
                                             vmem_limit_bytes=VMEM_LIMIT),
        name="ada_mod",
    )(conds, ada_w, ada_b.reshape(depth, 1, n6))


def _proj_kernel(x_ref, mod_ref, g_ref, wm_ref, wg_ref, pm_ref, pg_ref):
    x = x_ref[...]
    y = x * lax.rsqrt(jnp.mean(x * x, axis=-1, keepdims=True) + EPS)
    h = (y * g_ref[...]) * (1.0 + mod_ref[0, 1:2, :]) + mod_ref[0, 0:1, :]
    hb = h.astype(BF16)
    pm_ref[...] = jnp.dot(hb, wm_ref[...], preferred_element_type=F32)
    pg_ref[...] = jnp.dot(hb, wg_ref[...], preferred_element_type=F32)


def _mod_map(n_ctx_tiles, tiles_per_latent_seq):
    def index(i, *_):
        return (jnp.where(i < n_ctx_tiles, 0, 1 + (i - n_ctx_tiles) // tiles_per_latent_seq), 0, 0)
    return index


def _proj(x, mod, g, w_main, w_gate, mod_map, tm):
    t, d = x.shape
    const = lambda i: (0, 0)
    return pl.pallas_call(
        _proj_kernel,
        grid=(t // tm,),
        in_specs=[pl.BlockSpec((tm, d), lambda i: (i, 0)),
                  pl.BlockSpec((1, 8, d), mod_map),
                  pl.BlockSpec((1, d), const),
                  pl.BlockSpec(w_main.shape, const),
                  pl.BlockSpec(w_gate.shape, const)],
        out_specs=[pl.BlockSpec((tm, P_MAIN), lambda i: (i, 0)),
                   pl.BlockSpec((tm, w_gate.shape[1]), lambda i: (i, 0))],
        out_shape=[jax.ShapeDtypeStruct((t, P_MAIN), F32),
                   jax.ShapeDtypeStruct((t, w_gate.shape[1]), F32)],
        compiler_params=pltpu.CompilerParams(dimension_semantics=("arbitrary",),
                                             vmem_limit_bytes=VMEM_LIMIT),
        name="in_proj",
    )(x, mod, g, w_main, w_gate)


def _mixer_direction(d, c_idx, n_chunks, use_rope, pm, hprev, hnext, cos_ref, sin_ref,
                     small_b, a_row, ret_rep, wg, gb, conv_w, conv_b, d_row, spread_b, spread_dt,
                     o_ref, mS, mm, rS, gS, sS, lat_s, cst_s):
    C = CHUNK
    last = C - 1 if d == 0 else 0
    row = lax.broadcasted_iota(jnp.int32, (C, C), 0)
    col = lax.broadcasted_iota(jnp.int32, (C, C), 1)
    mask = (col <= row) if d == 0 else (col >= row)
    mask_t = (row <= col) if d == 0 else (row >= col)
    tri = mask.astype(BF16)
    tri_t_ones = jnp.concatenate([mask_t.astype(BF16), jnp.ones((C, C), BF16)], axis=1)
    neg_inf = jnp.float32(-jnp.inf)
    lane = lax.broadcasted_iota(jnp.int32, (1, LANES), 1)
    even = lane < 64
    pair_sel = (even, jnp.logical_not(even))
    rows_of = (slice(0, 64), slice(64, 128))

    small = pm[:, SMALL:SMALL + LANES]
    pre = small + small_b[...]
    sp = _softplus(pre)
    lsg = -_softplus(-pre)
    a_neg = -jnp.exp(a_row[...])
    la = jnp.where(lane < S_MF, pre,
                   jnp.where(lane < S_DT, lsg,
                             jnp.where(lane < S_DTRAW, sp * a_neg, 0.0)))
    la_t = la.T
    lat_s[d] = la_t
    cst_s[d] = _dot_mask(la_t, tri_t_ones)
    cs = _mask_dot(tri, la)
    b_spread = _dot_spread(cs, spread_b[d])
    dt_spread = _dot_spread(sp, spread_dt[d])

    q_all = pm[:, MQ:MQ + 256] * (64.0 ** -0.5)
    k_all = pm[:, MK:MK + 256]
    v_all = pm[:, MV:MV + 256]
    k_t = k_all.T
    ones_blk = jnp.ones((C, LANES), BF16)
    for p in range(2):
        ps = slice(128 * p, 128 * p + 128)
        q_pair = q_all[:, ps].astype(BF16)
        k_pair = k_all[:, ps]
        vx = jnp.concatenate([v_all[:, ps].astype(BF16), ones_blk], axis=1)
        outs = []
        for r in range(2):
            h = 2 * p + r
            ci = d * 4 + h
            b_full = b_spread[:, 128 * h:128 * h + 128]
            b_row = cst_s[d, S_MF + ci:S_MF + ci + 1, 0:C]
            bl_row = cst_s[d, S_MF + ci:S_MF + ci + 1, C:2 * C]
            u_row = lat_s[d, S_MI + ci:S_MI + ci + 1, :] - b_row
            ms_row = mm[0, ci:ci + 1, :]
            d_log = jnp.where(mask, b_full + u_row, neg_inf)
            inter = b_full + ms_row
            m_t = jnp.maximum(inter, jnp.max(d_log, axis=1, keepdims=True))
            w = jnp.exp(d_log - m_t)
            k_m = jnp.where(pair_sel[r], k_pair, 0.0)
            sc = _dot_nt(q_pair, k_m) * w
            w_inter = jnp.exp(inter - m_t)
            s_old = mS[0, d, h]
            intra = _dot(sc, vx)
            cross = _dot(q_pair, s_old)
            num = intra[:, 0:LANES] + w_inter * cross[:, 0:LANES]
            den = intra[:, LANES:] + w_inter * cross[:, LANES:]
            outs.append(num / jnp.maximum(jnp.abs(den), jnp.exp(-m_t)))
            g_row = bl_row + u_row
            m_new = jnp.maximum(bl_row + ms_row, jnp.max(g_row, axis=1, keepdims=True))
            wk_row = jnp.exp(g_row - m_new)
            decay = jnp.exp(bl_row + ms_row - m_new)
            upd = _dot(k_t[64 * h:64 * h + 64, :] * wk_row, vx)
            live = rows_of[r]
            mS[0, d, h, live, 0:LANES] = decay * s_old[live, 0:LANES] + upd[:, 0:LANES]
            mS[0, d, h, live, LANES:] = decay * s_old[live, LANES:] + upd[:, LANES:]
            mm[0, ci:ci + 1, :] = m_new
        o_ref[:, ps] = jnp.where(even, outs[0], outs[1])

    rq = pm[:, RQ:RQ + 256]
    rk = pm[:, RK:RK + 256] * (64.0 ** -0.5)
    rv = pm[:, RV:RV + 256]
    if use_rope:
        cosv = cos_ref[...]
        sinv = sin_ref[...]
        first = (lax.broadcasted_iota(jnp.int32, (1, 256), 1) % 64) < 32

        def rope(x):
            rot = jnp.where(first, pltpu.roll(x, 256 - 32, 1), pltpu.roll(x, 32, 1))
            return x * cosv + rot * sinv

        rq = rope(rq)
        rk = rope(rk)
    rk_t = rk.T
    adiff = jnp.abs(row - col).astype(F32)
    ri_f = lax.broadcasted_iota(jnp.int32, (C, LANES), 0).astype(F32)
    cj_f = lax.broadcasted_iota(jnp.int32, (1, C), 1).astype(F32)
    ret_la = -jnp.exp(ret_rep[...])
    for p in range(2):
        ps = slice(128 * p, 128 * p + 128)
        q_pair = rq[:, ps].astype(BF16)
        k_pair = rk[:, ps]
        v_pair = rv[:, ps].astype(BF16)
        outs = []
        for r in range(2):
            h = 2 * p + r
            ci = d * 4 + h
            la_h = ret_la[ci:ci + 1, :]
            dec = jnp.where(mask, jnp.exp(la_h * adiff), 0.0)
            if d == 0:
                eb = jnp.exp(la_h * (ri_f + 1.0))
                kw_row = jnp.exp(la_h * (float(C - 1) - cj_f))
            else:
                eb = jnp.exp(la_h * (float(C) - ri_f))
                kw_row = jnp.exp(la_h * cj_f)
            k_m = jnp.where(pair_sel[r], k_pair, 0.0)
            sc = _dot_nt(q_pair, k_m) * dec
            s_old = rS[0, d, h]
            outs.append(_dot(sc, v_pair) + eb * _dot(q_pair, s_old))
            live = rows_of[r]
            rS[0, d, h, live, :] = (jnp.exp(la_h * float(C)) * s_old[live, :]
                                    + _dot(rk_t[64 * h:64 * h + 64, :] * kw_row, v_pair))
        o_ref[:, 256 + 128 * p:256 + 128 * p + 128] = jnp.where(even, outs[0], outs[1])

    gate = _dot(small, wg[d]) + gb[d]
    lga = (-_softplus(-gate)) * (1.0 / GLA_TAU)
    b = _mask_dot(tri, lga)
    bl_row = b[last:last + 1, :]
    b_mid = b[C // 2:C // 2 + 1, :]
    gq = pm[:, GQ:GQ + 128]
    gk = pm[:, GK:GK + 128] * (32.0 ** -0.5)
    gv = pm[:, GV:GV + 256]
    qc = (gq * jnp.exp(b - b_mid)).astype(BF16)
    kc = gk * jnp.exp(b_mid - b)
    qe = (gq * jnp.exp(b)).astype(BF16)
    kd = gk * jnp.exp(bl_row - b)
    ebl = jnp.exp(bl_row)
    gv_t = gv.T
    key_head = lane // 32
    for p in range(2):
        v_pair = gv[:, 128 * p:128 * p + 128].astype(BF16)
        outs = []
        for r in range(2):
            h = 2 * p + r
            mine = key_head == h
            sc = jnp.where(mask, _dot_nt(qc, jnp.where(mine, kc, 0.0)), 0.0)
            st_old = gS[0, d, h]
            outs.append(_dot(sc, v_pair) + _dot_nt(qe, st_old))
            live = rows_of[r]
            gS[0, d, h, live, :] = (st_old[live, :] * ebl
                                    + _dot(gv_t[64 * h:64 * h + 64, :], jnp.where(mine, kd, 0.0)))
        o_ref[:, 512 + 128 * p:512 + 128 * p + 128] = jnp.where(even, outs[0], outs[1])

    xr = pm[:, SXBC:SXBC + 512]
    prev = jnp.where(c_idx > 0, hprev[SUBLANES - 1:SUBLANES, :], 0.0)
    nxt = jnp.where(c_idx < n_chunks - 1, hnext[0:1, :], 0.0)
    rowi = lax.broadcasted_iota(jnp.int32, (C, 512), 0)
    x_dn = jnp.where(rowi == 0, prev, pltpu.roll(xr, 1, 0))
    x_up = jnp.where(rowi == C - 1, nxt, pltpu.roll(xr, C - 1, 0))
    y = conv_w[0:1, :] * x_dn + conv_w[1:2, :] * xr + conv_w[2:3, :] * x_up + conv_b[...]
    xbc = y * _sigmoid(y)
    sx = xbc[:, 0:256]
    s_b = xbc[:, 256:384]
    s_c = xbc[:, 384:512].astype(BF16)
    sb_t = s_b.T
    xdt = sx * dt_spread
    for p in range(2):
        ps = slice(128 * p, 128 * p + 128)
        v_pair = xdt[:, ps].astype(BF16)
        raw = _dot_nt(s_c, jnp.where(pair_sel[p], s_b, 0.0))
        outs = []
        for r in range(2):
            h = 2 * p + r
            ci = d * 4 + h
            b_full = b_spread[:, 512 + 128 * h:512 + 128 * h + 128]
            b_row = cst_s[d, S_DT + ci:S_DT + ci + 1, 0:C]
            bl_row = cst_s[d, S_DT + ci:S_DT + ci + 1, C:2 * C]
            dec = jnp.exp(jnp.where(mask, b_full - b_row, neg_inf))
            s_old = sS[0, d, h]
            outs.append(_dot(raw * dec, v_pair) + jnp.exp(b_full) * _dot(s_c, s_old))
            live = rows_of[p]
            sS[0, d, h, live, :] = (jnp.exp(bl_row) * s_old[live, :]
                                    + _dot(sb_t[64 * p:64 * p + 64, :] * jnp.exp(bl_row - b_row), v_pair))
        o = jnp.where(even, outs[0], outs[1])
        if d == 0:
            o = o + d_row[:, ps] * sx[:, ps]
        o_ref[:, 768 + 128 * p:768 + 128 * p + 128] = o


def _mixer_kernel(n_chunks, use_rope, *refs):
    it = iter(refs)
    pm_f, pm_b, hpf, hnf, hpb, hnb = (next(it) for _ in range(6))
    if use_rope:
        cos_f, sin_f, cos_b, sin_b = (next(it) for _ in range(4))
    else:
        cos_f = sin_f = cos_b = sin_b = None
    mS0, mm0, r0, g0, s0 = (next(it) for _ in range(5))
    params = tuple(next(it) for _ in range(10))
    o_f, o_b, mS, mm, rS, gS, sS = (next(it) for _ in range(7))
    lat_s, cst_s = (next(it) for _ in range(2))

    s = pl.program_id(1)

    @pl.when(s == 0)
    def _():
        mS[...] = mS0[...]
        mm[...] = mm0[...]
        rS[...] = r0[...]
        gS[...] = g0[...]
        sS[...] = s0[...]

    _mixer_direction(0, s, n_chunks, use_rope, pm_f, hpf, hnf, cos_f, sin_f, *params,
                     o_f, mS, mm, rS, gS, sS, lat_s, cst_s)
    _mixer_direction(1, n_chunks - 1 - s, n_chunks, use_rope, pm_b, hpb, hnb, cos_b, sin_b, *params,
                     o_b, mS, mm, rS, gS, sS, lat_s, cst_s)


def _mixers(pm, init, params, rope, n_seq, seq_len, row0):
    C = CHUNK
    n = seq_len // C
    t = pm.shape[0]
    chunk0 = row0 // C
    rb = C // SUBLANES
    n_rb = t // SUBLANES
    xbc_blk = SXBC // 512
    use_rope = rope is not None

    def f_idx(b, s):
        return chunk0 + b * n + s

    def b_idx(b, s):
        return chunk0 + b * n + (n - 1 - s)

    in_specs = [
        pl.BlockSpec((C, P_MAIN), lambda b, s: (f_idx(b, s), 0)),
        pl.BlockSpec((C, P_MAIN), lambda b, s: (b_idx(b, s), 0)),
        pl.BlockSpec((SUBLANES, 512), lambda b, s: (jnp.maximum(f_idx(b, s) * rb - 1, 0), xbc_blk)),
        pl.BlockSpec((SUBLANES, 512), lambda b, s: (jnp.minimum(f_idx(b, s) * rb + rb, n_rb - 1), xbc_blk)),
        pl.BlockSpec((SUBLANES, 512), lambda b, s: (jnp.maximum(b_idx(b, s) * rb - 1, 0), xbc_blk)),
        pl.BlockSpec((SUBLANES, 512), lambda b, s: (jnp.minimum(b_idx(b, s) * rb + rb, n_rb - 1), xbc_blk)),
    ]
    args = [pm, pm, pm, pm, pm, pm]
    if use_rope:
        cos, sin = rope
        in_specs += [pl.BlockSpec((C, 256), lambda b, s: (s, 0)),
                     pl.BlockSpec((C, 256), lambda b, s: (s, 0)),
                     pl.BlockSpec((C, 256), lambda b, s: (n - 1 - s, 0)),
                     pl.BlockSpec((C, 256), lambda b, s: (n - 1 - s, 0))]
        args += [cos, sin, cos, sin]

    def state_spec(shape):
        blk = (1,) + tuple(shape[1:])
        nd = len(shape)
        return pl.BlockSpec(blk, lambda b, s: (b,) + (0,) * (nd - 1))

    for a in init:
        in_specs.append(state_spec(a.shape))
        args.append(a)
    for p in params:
        nd = p.ndim
        in_specs.append(pl.BlockSpec(p.shape, lambda b, s, nd=nd: (0,) * nd))
        args.append(p)

    t_out = n_seq * seq_len
    out_specs = [pl.BlockSpec((C, 1024), lambda b, s: (b * n + s, 0)),
                 pl.BlockSpec((C, 1024), lambda b, s: (b * n + (n - 1 - s), 0))]
    out_shape = [jax.ShapeDtypeStruct((t_out, 1024), F32), jax.ShapeDtypeStruct((t_out, 1024), F32)]
    for a in init:
        out_specs.append(state_spec(a.shape))
        out_shape.append(jax.ShapeDtypeStruct(a.shape, F32))

    scratch = [pltpu.VMEM((2, LANES, C), F32), pltpu.VMEM((2, LANES, 2 * C), F32)]
    return pl.pallas_call(
        functools.partial(_mixer_kernel, n, use_rope),
        grid=(n_seq, n),
        in_specs=in_specs,
        out_specs=out_specs,
        out_shape=out_shape,
        scratch_shapes=scratch,
        compiler_params=pltpu.CompilerParams(dimension_semantics=("arbitrary", "arbitrary"),
                                             vmem_limit_bytes=VMEM_LIMIT),
        name="mixers_rope" if use_rope else "mixers",
    )(*args)


def _post_kernel(n_experts, n_ctx_tiles, ofc_ref, obc_ref, ofl_ref, obl_ref, pg_ref, x_ref, mod_ref,
                 hg_ref, bd_ref, wo_ref, n2_ref, rw_ref, rb_ref, ltri_ref,
                 x1_ref, h2_ref, ei_ref, gt_ref, rk_ref, cnt_ref):
    @pl.when(pl.program_id(0) == 0)
    def _():
        cnt_ref[...] = jnp.zeros_like(cnt_ref)

    y = jnp.where(pl.program_id(0) < n_ctx_tiles, ofc_ref[...] + obc_ref[...], ofl_ref[...] + obl_ref[...])
    gt = pg_ref[...]
    lane = lax.broadcasted_iota(jnp.int32, (1, y.shape[1]), 1)
    sig = _sigmoid(gt)
    silu = gt * sig
    pre = y * jnp.where(lane >= 768, silu, 1.0)
    sq = (pre * pre).astype(BF16)
    ms = jnp.concatenate([jnp.dot(sq[:, LANES * s:LANES * s + LANES], bd_ref[...], preferred_element_type=F32)
                          for s in range(y.shape[1] // LANES)], axis=1) * (1.0 / 64.0)
    nrm = pre * lax.rsqrt(ms + EPS) * hg_ref[...]
    post = jnp.where(lane < 256, sig, jnp.where(lane < 768, silu, 1.0))
    mixed = jnp.dot((nrm * post).astype(BF16), wo_ref[...], preferred_element_type=F32)
    x1 = x_ref[...] + mod_ref[0, 2:3, :] * mixed
    x1_ref[...] = x1
    yn = x1 * lax.rsqrt(jnp.mean(x1 * x1, axis=-1, keepdims=True) + EPS)
    h2 = (yn * n2_ref[...]) * (1.0 + mod_ref[0, 4:5, :]) + mod_ref[0, 3:4, :]
    bits = lax.bitcast_convert_type(h2.astype(BF16).astype(F32), jnp.uint32)
    half = h2.shape[1] // 2
    for s in range(h2_ref.shape[0]):
        lo = bits[:, LANES * s:LANES * s + LANES]
        hi = bits[:, half + LANES * s:half + LANES * s + LANES]
        h2_ref[s] = lo | (hi >> 16)

    tm = h2.shape[0]
    lane_f = lax.broadcasted_iota(jnp.int32, (1, LANES), 1).astype(F32)
    neg_inf = jnp.float32(-jnp.inf)
    logits = _dot(h2, rw_ref[...]) + rb_ref[...]
    work = jnp.where(lane_f < float(n_experts), logits, neg_inf)
    vals, idxs, hots = [], [], []
    for _ in range(TOP_K):
        m = jnp.max(work, axis=1, keepdims=True)
        idx = jnp.min(jnp.where(work == m, lane_f, float(LANES)), axis=1, keepdims=True)
        hot = lane_f == idx
        vals.append(m)
        idxs.append(idx)
        hots.append(hot)
        work = jnp.where(hot, neg_inf, work)
    exps = [jnp.exp(v - vals[0]) for v in vals]
    denom = exps[0]
    for e in exps[1:]:
        denom = denom + e
    sel = hots[0].astype(F32)
    for hot in hots[1:]:
        sel = sel + hot.astype(F32)
    before = jnp.dot(ltri_ref[...], sel.astype(BF16), preferred_element_type=F32) + cnt_ref[...]
    ranks = [jnp.sum(jnp.where(hot, before, 0.0), axis=1, keepdims=True) for hot in hots]
    cnt_ref[...] = cnt_ref[...] + jnp.sum(sel, axis=0, keepdims=True)
    slot = lax.broadcasted_iota(jnp.int32, (tm, TOP_K), 1)

    def pack(cols):
        out = cols[TOP_K - 1]
        for k in range(TOP_K - 2, -1, -1):
            out = jnp.where(slot == k, cols[k], out)
        return out

    ei_ref[...] = pack(idxs).astype(jnp.int32)
    gt_ref[...] = pack([e / denom for e in exps])
    rk_ref[...] = pack(ranks).astype(jnp.int32)


def _post(o_ctx, o_lat, pg, x, mod, hg, bd, w_out, n2g, rw, rb, ltri, n_experts, mod_map, tm):
    t, d = x.shape
    n_slab = d // LANES
    nct = o_ctx[0].shape[0] // tm
    row = lambda i: (i, 0)
    const = lambda i: (0, 0)
    ctx_row = lambda i: (jnp.minimum(i, nct - 1), 0)
    lat_row = lambda i: (jnp.maximum(i - nct, 0), 0)
    return pl.pallas_call(
        functools.partial(_post_kernel, n_experts, nct),
        grid=(t // tm,),
        in_specs=[pl.BlockSpec((tm, d), ctx_row), pl.BlockSpec((tm, d), ctx_row),
                  pl.BlockSpec((tm, d), lat_row), pl.BlockSpec((tm, d), lat_row),
                  pl.BlockSpec((tm, d), row),
                  pl.BlockSpec((tm, d), row), pl.BlockSpec((1, 8, d), mod_map),
                  pl.BlockSpec((1, d), const), pl.BlockSpec((LANES, LANES), const), pl.BlockSpec((d, d), const),
                  pl.BlockSpec((1, d), const), pl.BlockSpec((d, LANES), const), pl.BlockSpec((1, LANES), const),
                  pl.BlockSpec((tm, tm), const)],
        out_specs=[pl.BlockSpec((tm, d), row), pl.BlockSpec((n_slab // 2, tm, LANES), lambda i: (0, i, 0)),
                   pl.BlockSpec((tm, TOP_K), row), pl.BlockSpec((tm, TOP_K), row),
                   pl.BlockSpec((tm, TOP_K), row), pl.BlockSpec((1, LANES), const)],
        out_shape=[jax.ShapeDtypeStruct((t, d), F32), jax.ShapeDtypeStruct((n_slab // 2, t, LANES), jnp.uint32),
                   jax.ShapeDtypeStruct((t, TOP_K), jnp.int32), jax.ShapeDtypeStruct((t, TOP_K), F32),
                   jax.ShapeDtypeStruct((t, TOP_K), jnp.int32), jax.ShapeDtypeStruct((1, LANES), F32)],
        compiler_params=pltpu.CompilerParams(dimension_semantics=("arbitrary",),
                                             vmem_limit_bytes=VMEM_LIMIT),
        name="post_mix",
    )(o_ctx[0], o_ctx[1], o_lat[0], o_lat[1], pg, x, mod, hg, bd, w_out, n2g, rw, rb, ltri)


def _dispatch_kernel(tm, dest_ref, h_ref, xb_in, xb_out, sem):
    del xb_in

    def row_copy(t, k):
        return pltpu.make_async_copy(h_ref.at[:, pl.ds(t, 1), :],
                                     xb_out.at[:, pl.ds(dest_ref[t * TOP_K + k], 1), :], sem)

    def start(t, c):
        for k in range(TOP_K):
            row_copy(t, k).start(priority=k % 2)
        return c

    def wait(t, c):
        for k in range(TOP_K):
            row_copy(t, k).wait()
        return c

    lax.fori_loop(0, tm, start, 0, unroll=8)
    lax.fori_loop(0, tm, wait, 0, unroll=8)


def _dispatch(h2, dest, xb_zero, tm):
    n_slab, t, _ = h2.shape
    return pl.pallas_call(
        functools.partial(_dispatch_kernel, tm),
        grid=(t // tm,),
        in_specs=[pl.BlockSpec((tm * TOP_K,), lambda i: (i,), memory_space=pltpu.SMEM),
                  pl.BlockSpec((n_slab, tm, LANES), lambda i: (0, i, 0)),
                  pl.BlockSpec(memory_space=pl.ANY)],
        out_specs=pl.BlockSpec(memory_space=pl.ANY),
        out_shape=jax.ShapeDtypeStruct(xb_zero.shape, xb_zero.dtype),
        scratch_shapes=[pltpu.SemaphoreType.DMA(())],
        input_output_aliases={2: 0},
        compiler_params=pltpu.CompilerParams(dimension_semantics=("arbitrary",),
                                             vmem_limit_bytes=VMEM_LIMIT),
        name="moe_dispatch",
    )(dest, h2, xb_zero)


def _moe_kernel(be_ref, nu_ref, xb_ref, wgu_ref, bgu_ref, wdn_ref, bdn_ref, o_ref, wgu_b, wdn_b):
    i = pl.program_id(0)
    d_ff = wdn_ref.shape[2]
    n_slab = xb_ref.shape[0]
    active = i < nu_ref[0]
    new_expert = jnp.logical_or(i == 0, be_ref[i] != be_ref[jnp.maximum(i - 1, 0)])

    @pl.when(jnp.logical_and(active, new_expert))
    def _():
        wgu_b[...] = wgu_ref[0, 0].astype(BF16)
        wdn_b[...] = wdn_ref[0, 0].astype(BF16)

    @pl.when(active)
    def _():
        words = [xb_ref[s] for s in range(n_slab)]
        lo = [lax.bitcast_convert_type(w & jnp.uint32(0xFFFF0000), F32).astype(BF16) for w in words]
        hi = [lax.bitcast_convert_type(w << 16, F32).astype(BF16) for w in words]
        x = jnp.concatenate(lo + hi, axis=1)
        gu = jnp.dot(x, wgu_b[...], preferred_element_type=F32) + bgu_ref[0, 0]
        gate = jnp.minimum(gu[:, :d_ff], SWIGLU_LIMIT)
        up = jnp.clip(gu[:, d_ff:], -SWIGLU_LIMIT, SWIGLU_LIMIT)
        act = gate * _sigmoid(SWIGLU_ALPHA * gate) * (up + 1.0)
        out = jnp.dot(act.astype(BF16), wdn_b[...], preferred_element_type=F32) + bdn_ref[0, 0]
        for s in range(o_ref.shape[0]):
            o_ref[s] = out[:, LANES * s:LANES * s + LANES]

    @pl.when(jnp.logical_not(active))
    def _():
        o_ref[...] = jnp.zeros_like(o_ref)


def _moe(xb, block_e, n_used, w_gu, b_gu, w_dn, b_dn, layer, bm):
    n_words, rows, _ = xb.shape
    depth, n_e, d, n_gu = w_gu.shape
    d_ff = w_dn.shape[2]
    n_slab = d // LANES
    grid_spec = pltpu.PrefetchScalarGridSpec(
        num_scalar_prefetch=2,
        grid=(rows // bm,),
        in_specs=[pl.BlockSpec((n_words, bm, LANES), lambda i, be, nu: (0, jnp.minimum(i, nu[0] - 1), 0)),
                  pl.BlockSpec((1, 1, d, n_gu), lambda i, be, nu: (layer, be[i], 0, 0)),
                  pl.BlockSpec((1, 1, 1, n_gu), lambda i, be, nu: (layer, be[i], 0, 0)),
                  pl.BlockSpec((1, 1, d_ff, d), lambda i, be, nu: (layer, be[i], 0, 0)),
                  pl.BlockSpec((1, 1, 1, d), lambda i, be, nu: (layer, be[i], 0, 0))],
        out_specs=pl.BlockSpec((n_slab, bm, LANES), lambda i, be, nu: (0, i, 0)),
        scratch_shapes=[pltpu.VMEM((d, n_gu), BF16), pltpu.VMEM((d_ff, d), BF16)],
    )
    return pl.pallas_call(
        _moe_kernel,
        grid_spec=grid_spec,
        out_shape=jax.ShapeDtypeStruct((n_slab, rows, LANES), F32),
        compiler_params=pltpu.CompilerParams(dimension_semantics=("arbitrary",),
                                             vmem_limit_bytes=VMEM_LIMIT),
        name="moe_ffn",
    )(block_e, n_used, xb, w_gu, b_gu.reshape(depth, n_e, 1, n_gu), w_dn, b_dn.reshape(depth, n_e, 1, d))


def _combine_kernel(final, tm, dest_ref, dest_next_ref, x1_ref, gt_ref, mod_ref, fg_ref, yb_ref,
                    o_ref, ybuf, sem):
    i = pl.program_id(0)
    n_tiles = pl.num_programs(0)
    slot = lax.rem(i, 2)

    def row_copy(idx_ref, sl, t, k):
        return pltpu.make_async_copy(yb_ref.at[:, pl.ds(idx_ref[t * TOP_K + k], 1), :],
                                     ybuf.at[sl, k, :, pl.ds(t, 1), :], sem.at[sl])

    def start_tile(idx_ref, sl):
        def body(t, c):
            for k in range(TOP_K):
                row_copy(idx_ref, sl, t, k).start(priority=k % 2)
            return c
        lax.fori_loop(0, tm, body, 0, unroll=8)

    @pl.when(i == 0)
    def _():
        start_tile(dest_ref, slot)

    @pl.when(i + 1 < n_tiles)
    def _():
        start_tile(dest_next_ref, 1 - slot)

    def wait(t, c):
        for k in range(TOP_K):
            row_copy(dest_ref, slot, t, k).wait()
        return c

    lax.fori_loop(0, tm, wait, 0, unroll=8)

    n_slab = ybuf.shape[2]
    gt = gt_ref[...]
    g_full = [jnp.broadcast_to(gt[:, k:k + 1], (tm, LANES)) for k in range(TOP_K)]
    slabs = []
    for s in range(n_slab):
        cs = slice(LANES * s, LANES * s + LANES)
        y = g_full[0] * ybuf[slot, 0, s]
        for k in range(1, TOP_K):
            y = y + g_full[k] * ybuf[slot, k, s]
        slabs.append(x1_ref[:, cs] + mod_ref[0, 5:6, cs] * y)
    x2 = jnp.concatenate(slabs, axis=1)
    if final:
        x2 = x2 * lax.rsqrt(jnp.mean(x2 * x2, axis=-1, keepdims=True) + EPS) * fg_ref[...]
    o_ref[...] = x2


def _combine(x1, yb, dest, gates, mod, final_g, final, mod_map, tm):
    t, d = x1.shape
    n_slab = d // LANES
    n_tiles = t // tm
    return pl.pallas_call(
        functools.partial(_combine_kernel, final, tm),
        grid=(n_tiles,),
        in_specs=[pl.BlockSpec((tm * TOP_K,), lambda i: (i,), memory_space=pltpu.SMEM),
                  pl.BlockSpec((tm * TOP_K,), lambda i: (jnp.minimum(i + 1, n_tiles - 1),),
                               memory_space=pltpu.SMEM),
                  pl.BlockSpec((tm, d), lambda i: (i, 0)),
                  pl.BlockSpec((tm, TOP_K), lambda i: (i, 0)),
                  pl.BlockSpec((1, 8, d), mod_map),
                  pl.BlockSpec((1, d), lambda i: (0, 0)),
                  pl.BlockSpec(memory_space=pl.ANY)],
        out_specs=pl.BlockSpec((tm, d), lambda i: (i, 0)),
        out_shape=jax.ShapeDtypeStruct((t, d), F32),
        scratch_shapes=[pltpu.VMEM((2, TOP_K, n_slab, tm, LANES), F32), pltpu.SemaphoreType.DMA((2,))],
        compiler_params=pltpu.CompilerParams(dimension_semantics=("arbitrary",),
                                             vmem_limit_bytes=VMEM_LIMIT),
        name="moe_combine",
    )(dest, dest, x1, gates, mod, final_g, yb)


def _rope_tables(n_tokens):
    rows = n_tokens // GRID_W
    r, c = jnp.meshgrid(jnp.arange(rows), jnp.arange(GRID_W), indexing='ij')
    nf = 16
    inv = ROPE_BASE ** (-jnp.arange(nf, dtype=F32) / nf)
    ang = jnp.concatenate([r.reshape(-1, 1).astype(F32) * inv, c.reshape(-1, 1).astype(F32) * inv], axis=-1)
    cos, sin = jnp.cos(ang), jnp.sin(ang)
    cos_t = jnp.tile(jnp.concatenate([cos, cos], axis=-1), (1, 4))
    sin_t = jnp.tile(jnp.concatenate([-sin, sin], axis=-1), (1, 4))
    return cos_t, sin_t


def _pad_lanes(a, width=LANES):
    return jnp.pad(a, [(0, 0)] * (a.ndim - 1) + [(0, width - a.shape[-1])])


def _spread_matrices():
    sb = np.zeros((2, LANES, 1024), np.float32)
    sdt = np.zeros((2, LANES, 256), np.float32)
    for d in range(2):
        for h in range(4):
            ci = d * 4 + h
            sb[d, S_MF + ci, 128 * h:128 * h + 128] = 1.0
            sb[d, S_DT + ci, 512 + 128 * h:512 + 128 * h + 128] = 1.0
            sdt[d, S_DT + ci, 64 * h:64 * h + 64] = 1.0
    return jnp.asarray(sb, dtype=BF16), jnp.asarray(sdt, dtype=BF16)


def _pad_states(c_m, n_m, m_m, s_ret, s_gla, s_ssd):
    bsz = c_m.shape[0]

    def place(a, row0, n_rows, col0, n_cols):
        r, c = a.shape[-2:]
        return jnp.pad(a, ((0, 0), (0, 0), (row0, n_rows - row0 - r), (col0, n_cols - col0 - c)))

    mS, rS, gS, sS = [], [], [], []
    for h in range(4):
        r, g = h % 2, h // 2
        n_rep = jnp.broadcast_to(n_m[:, :, h, :, None], (bsz, 2, 64, 128))
        mS.append(jnp.concatenate([place(c_m[:, :, h], 64 * r, 128, 64 * r, 128),
                                   place(n_rep, 64 * r, 128, 0, 128)], axis=-1))
        rS.append(place(s_ret[:, :, h], 64 * r, 128, 64 * r, 128))
        gS.append(place(jnp.swapaxes(s_gla[:, :, h], -1, -2), 64 * r, 128, 32 * h, 128))
        sS.append(place(s_ssd[:, :, h], 64 * g, 128, 64 * r, 128))
    stack = lambda xs: jnp.stack(xs, axis=2)
    mm = jnp.broadcast_to(m_m.reshape(bsz, 8, 1), (bsz, 8, LANES))
    return stack(mS), mm, stack(rS), stack(gS), stack(sS)


def _unpad_states(mS, mm, rS, gS, sS):
    bsz = mS.shape[0]
    c_m, n_m, s_ret, s_gla, s_ssd = [], [], [], [], []
    for h in range(4):
        r, g = h % 2, h // 2
        lv = slice(64 * r, 64 * r + 64)
        c_m.append(mS[:, :, h, lv, lv])
        n_m.append(mS[:, :, h, lv, 128])
        s_ret.append(rS[:, :, h, lv, lv])
        s_gla.append(jnp.swapaxes(gS[:, :, h, lv, 32 * h:32 * h + 32], -1, -2))
        s_ssd.append(sS[:, :, h, 64 * g:64 * g + 64, lv])
    stack = lambda xs: jnp.stack(xs, axis=2)
    return (stack(c_m), stack(n_m), mm[:, :, 0].reshape(bsz, 2, 4), stack(s_ret), stack(s_gla), stack(s_ssd))


def kernel(x_prompt, x_sample, c, state_mlstm_C, state_mlstm_n, state_mlstm_m, state_ret, state_gla, state_ssd,
           c_ctx, ada_w, ada_b, norm1_g, norm2_g, final_g, w_in, w_out,
           mlstm_i_b, mlstm_f_b, mlstm_norm_g, ret_decay, ret_norm_g,
           gla_gate_w, gla_gate_b, gla_norm_g,
           ssd_conv_w, ssd_conv_b, ssd_dt_b, ssd_A_log, ssd_D, ssd_norm_g,
           router_w, router_b, moe_w_gu, moe_b_gu, moe_w_down, moe_b_down):
    depth = w_in.shape[0]
    bp, lp, d = x_prompt.shape
    bs, ls, _ = x_sample.shape
    n_e = router_w.shape[-1]
    tm = 256
    bm = 512

    offs = np.cumsum([0, 256, 256, 256, 256, 8, 8, 256, 256, 256, 256, 128, 128, 256, 256, 32, 256, 512, 8])
    (o_mq, o_mk, o_mv, o_mo, o_mi, o_mf, o_rq, o_rk, o_rv, o_rg, o_gq, o_gk, o_gv, o_gr, o_glr,
     o_sz, o_sxbc, o_sdt, _) = [int(v) for v in offs]

    def cols(a, n):
        return w_in[:, :, a:a + n]

    small_w = jnp.concatenate([cols(o_mi, 8), cols(o_mf, 8), cols(o_sdt, 8),
                               jnp.zeros((depth, d, 8), F32), cols(o_glr, 32),
                               jnp.zeros((depth, d, LANES - 64), F32)], axis=-1)
    w_main = jnp.concatenate([cols(o_mq, 256), cols(o_mk, 256), cols(o_mv, 256),
                              cols(o_rq, 256), cols(o_rk, 256), cols(o_rv, 256),
                              cols(o_gq, 128), cols(o_gk, 128), cols(o_gv, 256),
                              cols(o_sxbc, 512), small_w], axis=-1).astype(BF16)
    w_gate = jnp.concatenate([cols(o_mo, 256), cols(o_rg, 256), cols(o_gr, 256), cols(o_sz, 256)],
                             axis=-1).astype(BF16)
    w_out_b = w_out.astype(BF16)
    rw = _pad_lanes(router_w).astype(BF16)
    rb = _pad_lanes(router_b)[:, None, :]
    hg = jnp.concatenate([mlstm_norm_g, ret_norm_g, gla_norm_g, ssd_norm_g], axis=-1)[:, None, :]
    gidx = np.arange(LANES) // 64
    bd = jnp.asarray(gidx[:, None] == gidx[None, :], dtype=BF16)

    small_b = _pad_lanes(jnp.concatenate([mlstm_i_b.reshape(depth, 8), mlstm_f_b.reshape(depth, 8),
                                          ssd_dt_b.reshape(depth, 8)], axis=-1))[:, None, :]
    a_row = _pad_lanes(jnp.concatenate([jnp.zeros((depth, 16), F32), ssd_A_log.reshape(depth, 8)],
                                       axis=-1))[:, None, :]
    ret_rep = jnp.broadcast_to(ret_decay.reshape(depth, 8, 1), (depth, 8, LANES))
    wg = jnp.zeros((depth, 2, LANES, LANES), F32)
    wg = wg.at[:, 0, S_GLR:S_GLR + 16, :].set(gla_gate_w[:, 0])
    wg = wg.at[:, 1, S_GLR + 16:S_GLR + 32, :].set(gla_gate_w[:, 1])
    wg = wg.astype(BF16)
    gb = gla_gate_b[:, :, None, :]
    d_row = jnp.repeat(ssd_D, 64, axis=-1)[:, None, :]
    conv_b = ssd_conv_b[:, None, :]
    spread_b, spread_dt = _spread_matrices()

    n_cond = 1 + bs
    pad_rows = (-n_cond) % 8
    conds = jnp.concatenate([c_ctx[None, :], c, jnp.zeros((pad_rows, d), F32)], axis=0)
    mod_all = _ada_mod(conds, ada_w, ada_b).reshape(depth, n_cond + pad_rows, 6, d)
    mod_all = jnp.pad(mod_all, ((0, 0), (0, 0), (0, 2), (0, 0)))

    rope = _rope_tables(ls)
    zero = lambda *s: jnp.zeros(s, F32)
    init_ctx = (zero(bp, 2, 4, 128, 256), zero(bp, 8, LANES), zero(bp, 2, 4, 128, 128),
                zero(bp, 2, 4, 128, 128), zero(bp, 2, 4, 128, 128))

    t_ctx = bp * lp
    t_tot = t_ctx + bs * ls
    x = jnp.concatenate([x_prompt.reshape(t_ctx, d), x_sample.reshape(bs * ls, d)], axis=0)
    mod_map = _mod_map(t_ctx // tm, ls // tm)
    ltri = jnp.asarray(np.tril(np.ones((tm, tm), np.float32), -1), dtype=BF16)
    n_assign = t_tot * TOP_K
    n_blocks = n_assign // bm + n_e
    rows = n_blocks * bm
    finals = []
    for l in range(depth):
        params = (small_b[l], a_row[l], ret_rep[l], wg[l], gb[l], ssd_conv_w[l], conv_b[l], d_row[l],
                  spread_b, spread_dt)
        init_lat = _pad_states(state_mlstm_C[:, l], state_mlstm_n[:, l], state_mlstm_m[:, l],
                               state_ret[:, l], state_gla[:, l], state_ssd[:, l])
        pm, pg = _proj(x, mod_all[l], norm1_g[l][None, :], w_main[l], w_gate[l], mod_map, tm)
        res_ctx = _mixers(pm, init_ctx, params, None, bp, lp, 0)
        res_lat = _mixers(pm, init_lat, params, rope, bs, ls, t_ctx)
        finals.append(_unpad_states(*res_ctx[2:]))
        x1, h2, e_idx, gates, rank, counts = _post(
            res_ctx[:2], res_lat[:2], pg, x, mod_all[l], hg[l], bd, w_out_b[l], norm2_g[l][None, :],
            rw[l], rb[l], ltri, n_e, mod_map, tm)

        cnt = counts[0, :n_e].astype(jnp.int32)
        padded = (cnt + bm - 1) // bm * bm
        pad_end = jnp.cumsum(padded)
        pad_start = (pad_end - padded).astype(jnp.int32)
        block_start = jnp.arange(n_blocks, dtype=jnp.int32) * bm
        block_e = jnp.minimum(jnp.sum((pad_end[None, :] <= block_start[:, None]).astype(jnp.int32), axis=1),
                              n_e - 1).astype(jnp.int32)
        n_used = (pad_end[-1] // bm).astype(jnp.int32).reshape(1)
        experts = jnp.arange(n_e, dtype=jnp.int32)
        seg_start = jnp.sum(jnp.where(e_idx[..., None] == experts, pad_start, 0), axis=-1)
        dest = (seg_start + rank).astype(jnp.int32).reshape(n_assign)

        xb = _dispatch(h2, dest, jnp.zeros((h2.shape[0], rows, LANES), h2.dtype), tm)
        yb = _moe(xb, block_e, n_used, moe_w_gu, moe_b_gu, moe_w_down, moe_b_down, l, bm)
        x = _combine(x1, yb, dest, gates, mod_all[l], final_g[None, :], l == depth - 1, mod_map, tm)

    y_prompt = x[:t_ctx].reshape(bp, lp, d)
    y_sample = x[t_ctx:].reshape(bs, ls, d)
    outs = [jnp.stack([f[i] for f in finals], axis=1) for i in range(6)]
    return (y_prompt, y_sample, *outs)
```
